```python
import math
import jax
import jax.numpy as jnp
from jax import lax
import numpy as np

D_MODEL = 1024
BATCH = 2
SEQ = 8192
DEPTH = 4
DEC_BATCH = 128
DEC_SEQ = 8
PAST_LEN = 8192
PAGE_SIZE = 128

N_BRANCH = 4
BRANCH_W = D_MODEL // 4
MLA_V = 64
MLA_HEADS = BRANCH_W // MLA_V
MLA_NOPE = 64
MLA_ROPE = 32
MLA_Q_LORA = D_MODEL // 4
MLA_KV_LORA = D_MODEL // 8
ROPE_BASE = 10000.0
Q_BLOCK = 128
RWKV_HEAD = 64
RWKV_HEADS = BRANCH_W // RWKV_HEAD
RWKV_W_LORA = 32
RWKV_A_LORA = 32
RWKV_G_LORA = 64
RWKV_IN = 3 * BRANCH_W + RWKV_W_LORA + RWKV_A_LORA + RWKV_G_LORA
CONV_K = 31
S5_GROUP = 16
S5_GROUPS = BRANCH_W // S5_GROUP
S5_STATE = 64
D_FF = 2816
IN_SIZES = (MLA_Q_LORA, MLA_KV_LORA + MLA_ROPE, RWKV_IN, 2 * BRANCH_W, BRANCH_W, N_BRANCH * D_MODEL)
IN_W = MLA_Q_LORA + MLA_KV_LORA + MLA_ROPE + RWKV_IN + 2 * BRANCH_W + BRANCH_W + N_BRANCH * D_MODEL
ALPHA = (2 * DEPTH) ** 0.25
BETA = (8 * DEPTH) ** -0.25
ATTN_SCALE = (MLA_NOPE + MLA_ROPE) ** -0.5
LN_EPS = 1e-5
RMS_EPS = 1e-6
GN_EPS = 64e-5

kernel_name = 'hybrid_mla_rwkv7_conformer_s5_decoder_step'


def _split(x, sizes):
    outs, o = [], 0
    for s in sizes:
        outs.append(x[..., o:o + s])
        o += s
    return outs


def layer_norm(x, g, b):
    xf = x.astype(jnp.float32)
    mu = jnp.mean(xf, -1, keepdims=True)
    var = jnp.mean(jnp.square(xf - mu), -1, keepdims=True)
    return ((xf - mu) * lax.rsqrt(var + LN_EPS)).astype(x.dtype) * g + b


def rms_norm(x, g):
    xf = x.astype(jnp.float32)
    return (xf * lax.rsqrt(jnp.mean(jnp.square(xf), -1, keepdims=True) + RMS_EPS)).astype(x.dtype) * g


def swiglu(x, w_in, w_down):
    a, b = _split(x @ w_in, (D_FF, D_FF))
    return (jax.nn.silu(a) * b) @ w_down


def rope_tables(pos):
    inv = 1.0 / (ROPE_BASE ** (jnp.arange(0, MLA_ROPE, 2, dtype=jnp.float32) / MLA_ROPE))
    ang = pos.astype(jnp.float32)[:, None] * inv[None, :]
    return jnp.cos(ang), jnp.sin(ang)


def apply_rope(x, cos, sin):
    xf = x.astype(jnp.float32)
    x1, x2 = xf[..., :MLA_ROPE // 2], xf[..., MLA_ROPE // 2:]
    return jnp.concatenate([x1 * cos - x2 * sin, x1 * sin + x2 * cos], -1).astype(x.dtype)


def latent_attention(q_lat, q_pe, ckv, kpe, q_pos, k_pos):
    s = (jnp.einsum('bqhr,bkr->bhqk', q_lat, ckv, preferred_element_type=jnp.float32)
         + jnp.einsum('bqhp,bkp->bhqk', q_pe, kpe, preferred_element_type=jnp.float32)) * ATTN_SCALE
    s = jnp.where(k_pos[None, :] <= q_pos[:, None], s, -jnp.inf)
    p = jax.nn.softmax(s, axis=-1).astype(ckv.dtype)
    return jnp.einsum('bhqk,bkr->bqhr', p, ckv)


def mla_prompt_attention(q_lat, q_pe, ckv, kpe):
    B, T, H, R = q_lat.shape
    nb = T // Q_BLOCK
    pos = jnp.arange(T)

    def block(args):
        ql, qp, qpos = args
        return latent_attention(ql, qp, ckv, kpe, qpos, pos)

    ql = q_lat.reshape(B, nb, Q_BLOCK, H, R).swapaxes(0, 1)
    qp = q_pe.reshape(B, nb, Q_BLOCK, H, MLA_ROPE).swapaxes(0, 1)
    out = lax.map(block, (ql, qp, pos.reshape(nb, Q_BLOCK)))
    return out.swapaxes(0, 1).reshape(B, T, H, R)


def wkv7_scan(r, w, k, v, a, b, S0):
    def step(S, inp):
        r_t, w_t, k_t, v_t, a_t, b_t = inp
        sa = jnp.einsum('bhij,bhj->bhi', S, a_t)
        S = S * w_t[:, :, None, :] + sa[..., None] * b_t[:, :, None, :] + v_t[..., None] * k_t[:, :, None, :]
        return S, jnp.einsum('bhij,bhj->bhi', S, r_t)

    xs = tuple(jnp.swapaxes(t, 0, 1) for t in (r, w, k, v, a, b))
    S, ys = lax.scan(step, S0, xs)
    return jnp.swapaxes(ys, 0, 1), S


def rwkv7_time_mix(zr, shift0, S0, W, l):
    B, T, _ = zr.shape
    prev = jnp.concatenate([shift0[:, None, :].astype(zr.dtype), zr[:, :-1]], axis=1)
    zm = (zr + (prev - zr) * W['rwkv_mu'][l]).astype(jnp.float32)
    r, k, v, wd, ad, gd = _split(zm, (BRANCH_W, BRANCH_W, BRANCH_W, RWKV_W_LORA, RWKV_A_LORA, RWKV_G_LORA))

    def heads(t):
        return t.reshape(B, T, RWKV_HEADS, RWKV_HEAD)

    w = -jax.nn.softplus(-(W['rwkv_w0'][l] + jnp.tanh(wd) @ W['rwkv_w2'][l])) - 0.5
    a = jax.nn.sigmoid(W['rwkv_a0'][l] + ad @ W['rwkv_a2'][l])
    g = jax.nn.sigmoid(gd) @ W['rwkv_g2'][l]
    kk = heads(k * W['rwkv_k_k'][l])
    kk = kk / jnp.maximum(jnp.sqrt(jnp.sum(kk * kk, -1, keepdims=True)), 1e-12)
    k = heads(k * (1.0 + (a - 1.0) * W['rwkv_k_a'][l]))
    r, v, a = heads(r), heads(v), heads(a)
    decay = heads(jnp.exp(-jnp.exp(w)))
    y, S = wkv7_scan(r, decay, k, v, -kk, kk * a, S0.astype(jnp.float32))
    mu = jnp.mean(y, -1, keepdims=True)
    var = jnp.mean(jnp.square(y - mu), -1, keepdims=True)
    y = ((y - mu) * lax.rsqrt(var + GN_EPS)).reshape(B, T, BRANCH_W) * W['rwkv_ln_g'][l] + W['rwkv_ln_b'][l]
    y = y + (jnp.sum(r * k * W['rwkv_r_k'][l], -1, keepdims=True) * v).reshape(B, T, BRANCH_W)
    return (y * g).astype(zr.dtype), S, zr[:, -1]


def conv_module(zc, buf, W, l):
    u = zc[..., :BRANCH_W] * jax.nn.sigmoid(zc[..., BRANCH_W:])
    upad = jnp.concatenate([buf.astype(u.dtype), u], axis=1)
    rhs = W['conv_w'][l].astype(u.dtype)[:, None, :]
    y = lax.conv_general_dilated(upad, rhs, (1,), 'VALID', dimension_numbers=('NWC', 'WIO', 'NWC'),
                                 feature_group_count=BRANCH_W)
    y = y + W['conv_b'][l]
    y = jax.nn.silu(layer_norm(y, W['conv_ln_g'][l], W['conv_ln_b'][l]))
    return y, upad[:, upad.shape[1] - (CONV_K - 1):]


def _ssm_combine(e1, e2):
    a1, b1 = e1
    a2, b2 = e2
    return a1 * a2, a2 * b1 + b2


def s5_ssm(u, h0_re, h0_im, W, l):
    B, T, _ = u.shape
    f32 = jnp.float32
    A = lax.complex(W['s5_a_re'][l].astype(f32), W['s5_a_im'][l].astype(f32))
    dt = jnp.exp(W['s5_log_dt'][l].astype(f32))[:, None]
    A_bar = jnp.exp(A * dt)
    B_c = lax.complex(W['s5_b_re'][l].astype(f32), W['s5_b_im'][l].astype(f32))
    C_c = lax.complex(W['s5_c_re'][l].astype(f32), W['s5_c_im'][l].astype(f32))
    B_bar = ((A_bar - 1.0) / A)[..., None] * B_c
    uf = u.astype(f32)
    bu = jnp.einsum('gpc,btgc->btgp', B_bar, uf.reshape(B, T, S5_GROUPS, S5_GROUP).astype(jnp.complex64))
    h0 = lax.complex(h0_re.astype(f32), h0_im.astype(f32))
    bu = bu.at[:, 0].add(A_bar * h0)
    _, h = lax.associative_scan(_ssm_combine, (jnp.broadcast_to(A_bar, bu.shape), bu), axis=1)
    y = jnp.einsum('gcp,btgp->btgc', C_c, h).real.reshape(B, T, BRANCH_W) + W['s5_d'][l] * uf
    y = jax.nn.gelu(y)
    y = y * jax.nn.sigmoid(y @ W['s5_glu_w'][l] + W['s5_glu_b'][l])
    h_last = h[:, -1]
    return y.astype(u.dtype), jnp.real(h_last), jnp.imag(h_last)


def token_mixer(h, cos, sin, W, l, st, mla_past, past_len):
    B, T, _ = h.shape
    rwkv_s0, shift0, conv0, s5_re0, s5_im0 = st
    zq, zkv, zr, zc, zs, zg = _split(h @ W['w_in'][l], IN_SIZES)
    q = (rms_norm(zq, W['mla_q_norm'][l]) @ W['mla_w_uq'][l]).reshape(B, T, MLA_HEADS, MLA_NOPE + MLA_ROPE)
    q_pe = apply_rope(q[..., MLA_NOPE:], cos[:, None, :], sin[:, None, :])
    ckv = rms_norm(zkv[..., :MLA_KV_LORA], W['mla_kv_norm'][l])
    kpe = apply_rope(zkv[..., MLA_KV_LORA:], cos, sin)
    q_lat = jnp.einsum('bthn,rhn->bthr', q[..., :MLA_NOPE], W['mla_w_uk'][l])
    if mla_past is None:
        o_lat = mla_prompt_attention(q_lat, q_pe, ckv, kpe)
    else:
        past_ckv, past_kpe = mla_past
        keys_c = jnp.concatenate([past_ckv.astype(ckv.dtype), ckv], axis=1)
        keys_p = jnp.concatenate([past_kpe.astype(kpe.dtype), kpe], axis=1)
        o_lat = latent_attention(q_lat, q_pe, keys_c, keys_p, past_len + jnp.arange(T), jnp.arange(past_len + T))
    y_mla = jnp.einsum('bthr,rhv->bthv', o_lat, W['mla_w_uv'][l]).reshape(B, T, BRANCH_W)
    y_rwkv, rwkv_new, shift_new = rwkv7_time_mix(zr, shift0, rwkv_s0, W, l)
    y_conv, conv_new = conv_module(zc, conv0, W, l)
    y_s5, s5_re, s5_im = s5_ssm(zs, s5_re0, s5_im0, W, l)
    branches = jnp.stack([y_mla, y_rwkv, y_conv.astype(h.dtype), y_s5], axis=2)
    y_br = jnp.einsum('btnc,ncd->btnd', branches, W['w_branch'][l])
    gates = jax.nn.sigmoid(zg.reshape(B, T, N_BRANCH, D_MODEL) + W['b_gate'][l])
    out = jnp.sum(gates * y_br, axis=2) @ W['w_out'][l]
    return out, (ckv, kpe, rwkv_new, shift_new, conv_new, s5_re, s5_im)


def decoder_layer(x, cos, sin, W, l, st, mla_past, past_len):
    x = layer_norm(ALPHA * x + 0.5 * swiglu(x, W['ffn1_w_in'][l], W['ffn1_w_down'][l]), W['ln1_g'][l], W['ln1_b'][l])
    m, new = token_mixer(x, cos, sin, W, l, st, mla_past, past_len)
    x = layer_norm(ALPHA * x + m, W['ln2_g'][l], W['ln2_b'][l])
    x = layer_norm(ALPHA * x + 0.5 * swiglu(x, W['ffn2_w_in'][l], W['ffn2_w_down'][l]), W['ln3_g'][l], W['ln3_b'][l])
    return x, new


def setup_inputs(seed: int = 0) -> dict:
    key = jax.random.key(seed)
    ks = iter(jax.random.split(key, 64))
    f32 = jnp.float32
    L = DEPTH

    def nrm(shape, scale):
        return jax.random.normal(next(ks), shape, f32) * scale

    def gain(shape):
        return 1.0 + nrm(shape, 0.01)

    n_pages = PAST_LEN // PAGE_SIZE
    n_pool = (5 * DEC_BATCH * n_pages) // 4
    page_table = jax.random.permutation(next(ks), n_pool)[:DEC_BATCH * n_pages].reshape(DEC_BATCH, n_pages).astype(jnp.int32)
    s5_im_init = jnp.pi * jnp.arange(S5_STATE, dtype=f32)
    return {
        'x_prompt': nrm((BATCH, SEQ, D_MODEL), 1.0),
        'x_sample': nrm((DEC_BATCH, DEC_SEQ, D_MODEL), 1.0),
        'cache_ckv': nrm((L, n_pool, PAGE_SIZE, MLA_KV_LORA), 1.0),
        'cache_kpe': nrm((L, n_pool, PAGE_SIZE, MLA_ROPE), 1.0),
        'state_rwkv': nrm((L, DEC_BATCH, RWKV_HEADS, RWKV_HEAD, RWKV_HEAD), 0.1),
        'state_rwkv_shift': nrm((L, DEC_BATCH, RWKV_IN), 1.0),
        'state_conv': nrm((L, DEC_BATCH, CONV_K - 1, BRANCH_W), 0.5),
        'state_s5_re': nrm((L, DEC_BATCH, S5_GROUPS, S5_STATE), 0.1),
        'state_s5_im': nrm((L, DEC_BATCH, S5_GROUPS, S5_STATE), 0.1),
        'page_table': page_table,
        'ffn1_w_in': nrm((L, D_MODEL, 2 * D_FF), D_MODEL ** -0.5),
        'ffn1_w_down': nrm((L, D_FF, D_MODEL), BETA * D_FF ** -0.5),
        'ln1_g': gain((L, D_MODEL)),
        'ln1_b': nrm((L, D_MODEL), 0.01),
        'w_in': nrm((L, D_MODEL, IN_W), D_MODEL ** -0.5),
        'b_gate': nrm((L, N_BRANCH, D_MODEL), 0.01),
        'mla_q_norm': gain((L, MLA_Q_LORA)),
        'mla_w_uq': nrm((L, MLA_Q_LORA, MLA_HEADS * (MLA_NOPE + MLA_ROPE)), MLA_Q_LORA ** -0.5),
        'mla_kv_norm': gain((L, MLA_KV_LORA)),
        'mla_w_uk': nrm((L, MLA_KV_LORA, MLA_HEADS, MLA_NOPE), MLA_KV_LORA ** -0.5),
        'mla_w_uv': nrm((L, MLA_KV_LORA, MLA_HEADS, MLA_V), MLA_KV_LORA ** -0.5),
        'rwkv_mu': jax.random.uniform(next(ks), (L, RWKV_IN), f32),
        'rwkv_w0': -3.0 + nrm((L, BRANCH_W), 0.5),
        'rwkv_w2': nrm((L, RWKV_W_LORA, BRANCH_W), 0.1 * RWKV_W_LORA ** -0.5),
        'rwkv_a0': nrm((L, BRANCH_W), 0.1),
        'rwkv_a2': nrm((L, RWKV_A_LORA, BRANCH_W), 0.1 * RWKV_A_LORA ** -0.5),
        'rwkv_g2': nrm((L, RWKV_G_LORA, BRANCH_W), RWKV_G_LORA ** -0.5),
        'rwkv_k_k': 0.85 + nrm((L, BRANCH_W), 0.02),
        'rwkv_k_a': 1.0 + nrm((L, BRANCH_W), 0.02),
        'rwkv_r_k': nrm((L, RWKV_HEADS, RWKV_HEAD), 0.1),
        'rwkv_ln_g': gain((L, BRANCH_W)),
        'rwkv_ln_b': nrm((L, BRANCH_W), 0.01),
        'conv_w': nrm((L, CONV_K, BRANCH_W), CONV_K ** -0.5),
        'conv_b': nrm((L, BRANCH_W), 0.01),
        'conv_ln_g': gain((L, BRANCH_W)),
        'conv_ln_b': nrm((L, BRANCH_W), 0.01),
        's5_a_re': -0.5 + nrm((L, S5_GROUPS, S5_STATE), 0.01),
        's5_a_im': s5_im_init[None, None, :] + nrm((L, S5_GROUPS, S5_STATE), 0.01),
        's5_log_dt': jax.random.uniform(next(ks), (L, S5_GROUPS), f32, math.log(1e-3), math.log(1e-1)),
        's5_b_re': nrm((L, S5_GROUPS, S5_STATE, S5_GROUP), (2 * S5_GROUP) ** -0.5),
        's5_b_im': nrm((L, S5_GROUPS, S5_STATE, S5_GROUP), (2 * S5_GROUP) ** -0.5),
        's5_c_re': nrm((L, S5_GROUPS, S5_GROUP, S5_STATE), (2 * S5_STATE) ** -0.5),
        's5_c_im': nrm((L, S5_GROUPS, S5_GROUP, S5_STATE), (2 * S5_STATE) ** -0.5),
        's5_d': nrm((L, BRANCH_W), 1.0),
        's5_glu_w': nrm((L, BRANCH_W, BRANCH_W), BRANCH_W ** -0.5),
        's5_glu_b': nrm((L, BRANCH_W), 0.01),
        'w_branch': nrm((L, N_BRANCH, BRANCH_W, D_MODEL), BRANCH_W ** -0.5),
        'w_out': nrm((L, D_MODEL, D_MODEL), BETA * D_MODEL ** -0.5),
        'ln2_g': gain((L, D_MODEL)),
        'ln2_b': nrm((L, D_MODEL), 0.01),
        'ffn2_w_in': nrm((L, D_MODEL, 2 * D_FF), D_MODEL ** -0.5),
        'ffn2_w_down': nrm((L, D_FF, D_MODEL), BETA * D_FF ** -0.5),
        'ln3_g': gain((L, D_MODEL)),
        'ln3_b': nrm((L, D_MODEL), 0.01),
    }


def reference(x_prompt, x_sample, cache_ckv, cache_kpe, state_rwkv, state_rwkv_shift, state_conv,
              state_s5_re, state_s5_im, page_table,
              ffn1_w_in, ffn1_w_down, ln1_g, ln1_b, w_in, b_gate,
              mla_q_norm, mla_w_uq, mla_kv_norm, mla_w_uk, mla_w_uv,
              rwkv_mu, rwkv_w0, rwkv_w2, rwkv_a0, rwkv_a2, rwkv_g2, rwkv_k_k, rwkv_k_a, rwkv_r_k,
              rwkv_ln_g, rwkv_ln_b,
              conv_w, conv_b, conv_ln_g, conv_ln_b,
              s5_a_re, s5_a_im, s5_log_dt, s5_b_re, s5_b_im, s5_c_re, s5_c_im, s5_d, s5_glu_w, s5_glu_b,
              w_branch, w_out, ln2_g, ln2_b, ffn2_w_in, ffn2_w_down, ln3_g, ln3_b):
    W = dict(ffn1_w_in=ffn1_w_in, ffn1_w_down=ffn1_w_down, ln1_g=ln1_g, ln1_b=ln1_b, w_in=w_in, b_gate=b_gate,
             mla_q_norm=mla_q_norm, mla_w_uq=mla_w_uq, mla_kv_norm=mla_kv_norm, mla_w_uk=mla_w_uk, mla_w_uv=mla_w_uv,
             rwkv_mu=rwkv_mu, rwkv_w0=rwkv_w0, rwkv_w2=rwkv_w2, rwkv_a0=rwkv_a0, rwkv_a2=rwkv_a2, rwkv_g2=rwkv_g2,
             rwkv_k_k=rwkv_k_k, rwkv_k_a=rwkv_k_a, rwkv_r_k=rwkv_r_k, rwkv_ln_g=rwkv_ln_g, rwkv_ln_b=rwkv_ln_b,
             conv_w=conv_w, conv_b=conv_b, conv_ln_g=conv_ln_g, conv_ln_b=conv_ln_b,
             s5_a_re=s5_a_re, s5_a_im=s5_a_im, s5_log_dt=s5_log_dt, s5_b_re=s5_b_re, s5_b_im=s5_b_im,
             s5_c_re=s5_c_re, s5_c_im=s5_c_im, s5_d=s5_d, s5_glu_w=s5_glu_w, s5_glu_b=s5_glu_b,
             w_branch=w_branch, w_out=w_out, ln2_g=ln2_g, ln2_b=ln2_b,
             ffn2_w_in=ffn2_w_in, ffn2_w_down=ffn2_w_down, ln3_g=ln3_g, ln3_b=ln3_b)
    Bp, Tp, _ = x_prompt.shape
    Bs, Ts, _ = x_sample.shape
    n_pages = page_table.shape[1]
    past_len = n_pages * PAGE_SIZE
    cos_p, sin_p = rope_tables(jnp.arange(Tp))
    cos_s, sin_s = rope_tables(past_len + jnp.arange(Ts))
    hp, hs = x_prompt, x_sample
    new_p, new_s = [], []
    for l in range(DEPTH):
        st_p = (jnp.zeros((Bp, RWKV_HEADS, RWKV_HEAD, RWKV_HEAD), jnp.float32),
                jnp.zeros((Bp, RWKV_IN), hp.dtype),
                jnp.zeros((Bp, CONV_K - 1, BRANCH_W), hp.dtype),
                jnp.zeros((Bp, S5_GROUPS, S5_STATE), jnp.float32),
                jnp.zeros((Bp, S5_GROUPS, S5_STATE), jnp.float32))
        hp, np_l = decoder_layer(hp, cos_p, sin_p, W, l, st_p, None, 0)
        new_p.append(np_l)
        past_ckv = cache_ckv[l, page_table].reshape(Bs, past_len, MLA_KV_LORA)
        past_kpe = cache_kpe[l, page_table].reshape(Bs, past_len, MLA_ROPE)
        st_s = (state_rwkv[l], state_rwkv_shift[l], state_conv[l], state_s5_re[l], state_s5_im[l])
        hs, ns_l = decoder_layer(hs, cos_s, sin_s, W, l, st_s, (past_ckv, past_kpe), past_len)
        new_s.append(ns_l)
    ckv_p, kpe_p, rwkv_p, shift_p, conv_p, s5re_p, s5im_p = [jnp.stack(t) for t in zip(*new_p)]
    ckv_s, kpe_s, rwkv_s, shift_s, conv_s, s5re_s, s5im_s = [jnp.stack(t) for t in zip(*new_s)]
    return (hp, hs, ckv_p, kpe_p, rwkv_p, shift_p, conv_p, s5re_p, s5im_p,
            ckv_s, kpe_s, rwkv_s, shift_s, conv_s, s5re_s, s5im_s)
```

```python
import functools
import math

import jax
import jax.numpy as jnp
from jax import lax
from jax.experimental import pallas as pl
from jax.experimental.pallas import tpu as pltpu

F32 = jnp.float32
BF16 = jnp.bfloat16
HI = lax.Precision.HIGHEST

D_MODEL = 1024
PAGE = 128
BW = D_MODEL // 4
N_BRANCH = 4
MLA_V = 64
MLA_H = BW // MLA_V
MLA_NOPE = 64
MLA_ROPE = 32
MLA_Q = D_MODEL // 4
MLA_KV = D_MODEL // 8
QK = MLA_KV + MLA_ROPE
ROPE_BASE = 10000.0
RW_HD = 64
RW_H = BW // RW_HD
RW_IN = 3 * BW + 128
CONV_K = 31
S5_G = 16
S5_GROUPS = BW // S5_G
S5_P = 64
S5_N = S5_GROUPS * S5_P
D_FF = 2816
FF_CHUNK = 256
DEPTH = 4
ALPHA = (2 * DEPTH) ** 0.25
ATTN_SCALE = (MLA_NOPE + MLA_ROPE) ** -0.5
LN_EPS = 1e-5
RMS_EPS = 1e-6
GN_EPS = 64e-5
VMEM_LIMIT = 56 * 1024 * 1024


def _pick(n, cands):
    for c in cands:
        if n % c == 0:
            return c
    raise ValueError(f"no tile in {cands} divides {n}")


def _const_spec(shape):
    nd = len(shape)
    return pl.BlockSpec(shape, lambda *_: (0,) * nd)


def _layer_spec(shape, l):
    nd = len(shape)
    return pl.BlockSpec((None,) + tuple(shape), lambda *_: (l,) + (0,) * nd)


def _params(sem, vmem=VMEM_LIMIT):
    return pltpu.CompilerParams(dimension_semantics=sem, vmem_limit_bytes=vmem)


def _dot(a, b, **kw):
    return jnp.dot(a, b, preferred_element_type=F32, **kw)


def _dot_nt(a, b, **kw):
    return lax.dot_general(a, b, (((1,), (1,)), ((), ())), preferred_element_type=F32, **kw)


def _dot_tn(a, b, **kw):
    return lax.dot_general(a, b, (((0,), (0,)), ((), ())), preferred_element_type=F32, **kw)


def _layer_norm(x, g, b):
    mu = jnp.mean(x, -1, keepdims=True)
    xc = x - mu
    var = jnp.mean(xc * xc, -1, keepdims=True)
    return xc * lax.rsqrt(var + LN_EPS) * g + b


def _rms_norm(x, g):
    return x * lax.rsqrt(jnp.mean(x * x, -1, keepdims=True) + RMS_EPS) * g


def _sigmoid(x):
    return 1.0 / (1.0 + jnp.exp(-x))


def _ffn_ln_kernel(x_ref, win_ref, wdn_ref, g_ref, b_ref, o_ref, acc_ref):
    x = x_ref[...]
    xb = x.astype(BF16)
    for c in range(D_FF // FF_CHUNK):
        lo = c * FF_CHUNK
        a = _dot(xb, win_ref[:, lo:lo + FF_CHUNK])
        b = _dot(xb, win_ref[:, D_FF + lo:D_FF + lo + FF_CHUNK])
        h = (a * _sigmoid(a) * b).astype(BF16)
        d = _dot(h, wdn_ref[lo:lo + FF_CHUNK, :])
        if c == 0:
            acc_ref[...] = d
        else:
            acc_ref[...] += d
    o_ref[...] = _layer_norm(ALPHA * x + 0.5 * acc_ref[...], g_ref[...], b_ref[...])


def _ffn_ln(x, w_in, w_down, g, b, l):
    n = x.shape[0]
    tm = _pick(n, (512, 256, 128, 64, 32, 16, 8))
    return pl.pallas_call(
        _ffn_ln_kernel,
        out_shape=jax.ShapeDtypeStruct((n, D_MODEL), F32),
        grid=(n // tm,),
        in_specs=[pl.BlockSpec((tm, D_MODEL), lambda i: (i, 0)),
                  _layer_spec((D_MODEL, 2 * D_FF), l), _layer_spec((D_FF, D_MODEL), l),
                  _layer_spec((1, D_MODEL), l), _layer_spec((1, D_MODEL), l)],
        out_specs=pl.BlockSpec((tm, D_MODEL), lambda i: (i, 0)),
        scratch_shapes=[pltpu.VMEM((tm, D_MODEL), F32)],
        compiler_params=_params(("parallel",)),
        name="ffn_ln",
    )(x, w_in, w_down, g, b)


def _prep_kernel(h_ref, cos_ref, sin_ref, wq_ref, wkv_ref, wkp_ref, wkps_ref, wr_ref, wc_ref, ws_ref,
                 gq_ref, gkv_ref, wn_ref, wp_ref, wps_ref, wuk_ref,
                 qcat_ref, ckv_ref, kpe_ref, kcat_ref, zr_ref, u_ref, zs_ref):
    hb = h_ref[...].astype(BF16)
    cos2 = cos_ref[...]
    sin2 = sin_ref[...]
    zq = _rms_norm(_dot(hb, wq_ref[...]), gq_ref[...]).astype(BF16)
    for h in range(MLA_H):
        qn = _dot(zq, wn_ref[h]).astype(BF16)
        qcat_ref[h, :, 0:MLA_KV] = _dot(qn, wuk_ref[h])
        qcat_ref[h, :, MLA_KV:QK] = _dot(zq, wp_ref[h]) * cos2 + _dot(zq, wps_ref[h]) * sin2
    ckv = _rms_norm(_dot(hb, wkv_ref[...]), gkv_ref[...])
    kpe = _dot(hb, wkp_ref[...]) * cos2 + _dot(hb, wkps_ref[...]) * sin2
    ckv_ref[...] = ckv
    kpe_ref[...] = kpe
    kcat_ref[:, 0:MLA_KV] = ckv.astype(BF16)
    kcat_ref[:, MLA_KV:QK] = kpe.astype(BF16)
    zr_ref[...] = _dot(hb, wr_ref[...])
    zc = _dot(hb, wc_ref[...])
    u_ref[...] = zc[:, :BW] * _sigmoid(zc[:, BW:])
    zs_ref[...] = _dot(hb, ws_ref[...])


def _prep(h, cos2, sin2, pw, l):
    n = h.shape[0]
    tm = _pick(n, (512, 256, 128, 64, 32, 16))
    row = lambda w: pl.BlockSpec((tm, w), lambda i: (i, 0))
    return pl.pallas_call(
        _prep_kernel,
        out_shape=(jax.ShapeDtypeStruct((MLA_H, n, QK), F32),
                   jax.ShapeDtypeStruct((n, MLA_KV), F32),
                   jax.ShapeDtypeStruct((n, MLA_ROPE), F32),
                   jax.ShapeDtypeStruct((n, QK), BF16),
                   jax.ShapeDtypeStruct((n, RW_IN), F32),
                   jax.ShapeDtypeStruct((n, BW), F32),
                   jax.ShapeDtypeStruct((n, BW), F32)),
        grid=(n // tm,),
        in_specs=[row(D_MODEL), row(MLA_ROPE), row(MLA_ROPE),
                  _layer_spec((D_MODEL, MLA_Q), l), _layer_spec((D_MODEL, MLA_KV), l),
                  _layer_spec((D_MODEL, MLA_ROPE), l), _layer_spec((D_MODEL, MLA_ROPE), l),
                  _layer_spec((D_MODEL, RW_IN), l), _layer_spec((D_MODEL, 2 * BW), l),
                  _layer_spec((D_MODEL, BW), l),
                  _layer_spec((1, MLA_Q), l), _layer_spec((1, MLA_KV), l),
                  _layer_spec((MLA_H, MLA_Q, MLA_NOPE), l), _layer_spec((MLA_H, MLA_Q, MLA_ROPE), l),
                  _layer_spec((MLA_H, MLA_Q, MLA_ROPE), l), _layer_spec((MLA_H, MLA_NOPE, MLA_KV), l)],
        out_specs=(pl.BlockSpec((MLA_H, tm, QK), lambda i: (0, i, 0)),
                   row(MLA_KV), row(MLA_ROPE), row(QK), row(RW_IN), row(BW), row(BW)),
        compiler_params=_params(("parallel",)),
        name="mixer_prep",
    )(h, cos2, sin2, pw["wq"], pw["wkv"], pw["wkp"], pw["wkps"], pw["wr"], pw["wc"], pw["ws"],
      pw["gq"], pw["gkv"], pw["wn"], pw["wp"], pw["wps"], pw["wuk"])


def _attn_prompt_kernel(q_ref, k_ref, wuv_ref, o_ref, m_ref, l_ref, acc_ref, *, tq):
    i = pl.program_id(1)
    rows = MLA_H * tq
    q = q_ref[...].reshape(rows, QK).astype(BF16)
    m_ref[...] = jnp.full((rows, 1), -jnp.inf, F32)
    l_ref[...] = jnp.zeros((rows, 1), F32)
    acc_ref[...] = jnp.zeros((rows, MLA_KV), F32)

    def update(kb, diagonal):
        k = k_ref[pl.ds(pl.multiple_of(kb * tq, tq), tq), :]
        s = _dot_nt(q, k) * ATTN_SCALE
        if diagonal:
            t = lax.broadcasted_iota(jnp.int32, (rows, tq), 0) % tq
            c = lax.broadcasted_iota(jnp.int32, (rows, tq), 1)
            s = jnp.where(c <= t, s, -jnp.inf)
        m_prev = m_ref[...]
        m_new = jnp.maximum(m_prev, jnp.max(s, -1, keepdims=True))
        alpha = jnp.exp(m_prev - m_new)
        p = jnp.exp(s - m_new)
        l_ref[...] = alpha * l_ref[...] + jnp.sum(p, -1, keepdims=True)
        acc_ref[...] = alpha * acc_ref[...] + _dot(p.astype(BF16), k[:, :MLA_KV])
        m_ref[...] = m_new

    def body(kb, carry):
        update(kb, False)
        return carry

    lax.fori_loop(0, i, body, 0)
    update(i, True)
    o = (acc_ref[...] / l_ref[...]).astype(BF16)
    y = _dot(o[0:tq], wuv_ref[0])
    for h in range(1, MLA_H):
        y += _dot(o[h * tq:(h + 1) * tq], wuv_ref[h])
    o_ref[...] = y


def _attn_prompt(qcat, kcat, wuv, l, bp, t):
    tq = _pick(t, (256, 128))
    nq = t // tq
    return pl.pallas_call(
        functools.partial(_attn_prompt_kernel, tq=tq),
        out_shape=jax.ShapeDtypeStruct((bp * t, BW), F32),
        grid=(bp, nq),
        in_specs=[pl.BlockSpec((MLA_H, tq, QK), lambda b, i: (0, b * nq + i, 0)),
                  pl.BlockSpec((t, QK), lambda b, i: (b, 0)),
                  _layer_spec((MLA_H, MLA_KV, BW), l)],
        out_specs=pl.BlockSpec((tq, BW), lambda b, i: (b * nq + i, 0)),
        scratch_shapes=[pltpu.VMEM((MLA_H * tq, 1), F32), pltpu.VMEM((MLA_H * tq, 1), F32),
                        pltpu.VMEM((MLA_H * tq, MLA_KV), F32)],
        compiler_params=_params(("parallel", "parallel")),
        name="mla_prompt",
    )(qcat, kcat, wuv)


def _attn_sample_kernel(pt_ref, q_ref, cn_ref, pn_ref, ckv_hbm, kpe_hbm, wuv_ref, o_ref,
                        kbuf, pbuf, sem, *, l, ts, n_pages):
    b = pl.program_id(0)
    slot = b % 2

    def page_copies(bb, p, sl):
        pg = pt_ref[bb, p]
        dst = pl.ds(pl.multiple_of(p * PAGE, PAGE), PAGE)
        return (pltpu.make_async_copy(ckv_hbm.at[l, pg], kbuf.at[sl, dst, :], sem.at[0, sl]),
                pltpu.make_async_copy(kpe_hbm.at[l, pg], pbuf.at[sl, dst, :], sem.at[1, sl]))

    def start_fetch(bb, sl):
        def body(p, carry):
            for cp in page_copies(bb, p, sl):
                cp.start()
            return carry
        lax.fori_loop(0, n_pages, body, 0)

    @pl.when(b == 0)
    def _():
        start_fetch(0, 0)

    @pl.when(b + 1 < pl.num_programs(0))
    def _():
        start_fetch(b + 1, 1 - slot)

    def wait_body(p, carry):
        for cp in page_copies(b, p, slot):
            cp.wait()
        return carry
    lax.fori_loop(0, n_pages, wait_body, 0)

    rows = MLA_H * ts
    q = q_ref[...].reshape(rows, QK)
    ql = q[:, :MLA_KV].astype(BF16)
    qp = q[:, MLA_KV:].astype(BF16)
    kc = kbuf[slot].astype(BF16)
    kp = pbuf[slot].astype(BF16)
    cn = cn_ref[...].astype(BF16)
    pn = pn_ref[...].astype(BF16)
    s = (_dot_nt(ql, kc) + _dot_nt(qp, kp)) * ATTN_SCALE
    sn = (_dot_nt(ql, cn) + _dot_nt(qp, pn)) * ATTN_SCALE
    t = lax.broadcasted_iota(jnp.int32, (rows, ts), 0) % ts
    c = lax.broadcasted_iota(jnp.int32, (rows, ts), 1)
    sn = jnp.where(c <= t, sn, -jnp.inf)
    m = jnp.maximum(jnp.max(s, -1, keepdims=True), jnp.max(sn, -1, keepdims=True))
    p = jnp.exp(s - m)
    pnew = jnp.exp(sn - m)
    den = jnp.sum(p, -1, keepdims=True) + jnp.sum(pnew, -1, keepdims=True)
    o = ((_dot(p.astype(BF16), kc) + _dot(pnew.astype(BF16), cn)) / den).astype(BF16)
    y = _dot(o[0:ts], wuv_ref[0])
    for h in range(1, MLA_H):
        y += _dot(o[h * ts:(h + 1) * ts], wuv_ref[h])
    o_ref[...] = y


def _attn_sample(page_table, qcat, ckv, kpe, cache_ckv, cache_kpe, wuv, l, row0, bs, ts):
    n_pages = page_table.shape[1]
    past = n_pages * PAGE
    blk0 = row0 // ts
    grid_spec = pltpu.PrefetchScalarGridSpec(
        num_scalar_prefetch=1,
        grid=(bs,),
        in_specs=[pl.BlockSpec((MLA_H, ts, QK), lambda b, pt: (0, blk0 + b, 0)),
                  pl.BlockSpec((ts, MLA_KV), lambda b, pt: (blk0 + b, 0)),
                  pl.BlockSpec((ts, MLA_ROPE), lambda b, pt: (blk0 + b, 0)),
                  pl.BlockSpec(memory_space=pl.ANY), pl.BlockSpec(memory_space=pl.ANY),
                  pl.BlockSpec((None, MLA_H, MLA_KV, BW), lambda b, pt: (l, 0, 0, 0))],
        out_specs=pl.BlockSpec((ts, BW), lambda b, pt: (b, 0)),
        scratch_shapes=[pltpu.VMEM((2, past, MLA_KV), F32), pltpu.VMEM((2, past, MLA_ROPE), F32),
                        pltpu.SemaphoreType.DMA((2, 2))])
    return pl.pallas_call(
        functools.partial(_attn_sample_kernel, l=l, ts=ts, n_pages=n_pages),
        out_shape=jax.ShapeDtypeStruct((bs * ts, BW), F32),
        grid_spec=grid_spec,
        compiler_params=_params(("arbitrary",)),
        name="mla_sample",
    )(page_table, qcat, ckv, kpe, cache_ckv, cache_kpe, wuv)


def _rwkv_kernel(zr_ref, sh0_ref, s0_ref, mu_ref, w0_ref, w2_ref, a0_ref, a2_ref, g2_ref, kk_ref, ka_ref,
                 rk_ref, lng_ref, lnb_ref, y_ref, s_ref, zb_ref, *, bb, c):
    j = pl.program_id(1)

    @pl.when(j == 0)
    def _():
        s_ref[...] = s0_ref[...]
        zb_ref[:, 7:8, :] = sh0_ref[...]

    cw = RW_H * c
    head_of_lane = lax.broadcasted_iota(jnp.int32, (1, BW), 1) // RW_HD
    bd_rows = (lax.broadcasted_iota(jnp.int32, (cw, BW), 0) // c
               == lax.broadcasted_iota(jnp.int32, (cw, BW), 1) // RW_HD)
    bd_sq = (lax.broadcasted_iota(jnp.int32, (cw, cw), 0) // c
             == lax.broadcasted_iota(jnp.int32, (cw, cw), 1) // c)
    bd_state = (lax.broadcasted_iota(jnp.int32, (BW, BW), 0) // RW_HD
                == lax.broadcasted_iota(jnp.int32, (BW, BW), 1) // RW_HD)
    ones_bd = bd_state.astype(F32)
    tt = lax.broadcasted_iota(jnp.int32, (c, cw), 0)
    ss = lax.broadcasted_iota(jnp.int32, (c, cw), 1) % c
    strict = ss < tt
    incl = ss <= tt
    tri = (lax.broadcasted_iota(jnp.int32, (c, c), 1)
           <= lax.broadcasted_iota(jnp.int32, (c, c), 0)).astype(F32)

    def tile_bd(x, mask):
        return jnp.where(mask, jnp.concatenate([x] * RW_H, axis=0), 0.0)

    for b in range(bb):
        z = zr_ref[b]
        zb_ref[b, 8:8 + c, :] = z
        prev = zb_ref[b, 7:7 + c, :]
        zb_ref[b, 7:8, :] = z[c - 1:c, :]
        zm = z + (prev - z) * mu_ref[...]
        r = zm[:, 0:BW]
        k = zm[:, BW:2 * BW]
        v = zm[:, 2 * BW:3 * BW]
        lo = zm[:, 3 * BW:]
        xw = -(w0_ref[...] + _dot(jnp.tanh(lo).astype(BF16), w2_ref[...]))
        softplus = jnp.maximum(xw, 0.0) + jnp.log(1.0 + jnp.exp(-jnp.abs(xw)))
        logw = -jnp.exp(-softplus - 0.5)
        a_sig = _sigmoid(a0_ref[...] + _dot(lo.astype(BF16), a2_ref[...]))
        g = _dot(_sigmoid(lo).astype(BF16), g2_ref[...])
        kk = k * kk_ref[...]
        kk = kk / jnp.maximum(jnp.sqrt(_dot(kk * kk, ones_bd, precision=HI)), 1e-12)
        k = k * (1.0 + (a_sig - 1.0) * ka_ref[...])
        a = -kk
        bm = kk * a_sig

        cl = _dot(tri, logw, precision=HI)
        cl_last = cl[c - 1:c, :]
        e_in = jnp.exp(cl)
        e_out = jnp.exp(-cl)
        e_end = jnp.exp(cl_last - cl)
        at = a * jnp.exp(cl - logw)
        rt = r * e_in
        kt = k * e_out
        bt = bm * e_out
        s_big = s_ref[b]
        ar = jnp.concatenate([at, rt], axis=0)
        kb_bd = jnp.concatenate([tile_bd(kt, bd_rows), tile_bd(bt, bd_rows)], axis=0)
        gm = _dot_nt(ar, kb_bd, precision=HI)
        l_ak = jnp.where(strict, gm[0:c, 0:cw], 0.0)
        l_ab = jnp.where(strict, gm[0:c, cw:2 * cw], 0.0)
        m_rk = jnp.where(incl, gm[c:2 * c, 0:cw], 0.0)
        m_rb = jnp.where(incl, gm[c:2 * c, cw:2 * cw], 0.0)
        xs = _dot_nt(ar, s_big, precision=HI)
        v_bd = tile_bd(v, bd_rows)
        x = xs[0:c] + _dot(l_ak, v_bd, precision=HI)
        lp = l_ab
        n_steps = max(1, (c - 1).bit_length())
        for it in range(n_steps):
            x = x + _dot(lp, tile_bd(x, bd_rows), precision=HI)
            if it + 1 < n_steps:
                lp = _dot(lp, tile_bd(lp, bd_sq), precision=HI)
        sa = x
        y = (xs[c:2 * c] + _dot(m_rk, v_bd, precision=HI)
             + _dot(m_rb, tile_bd(sa, bd_rows), precision=HI))
        vs = jnp.concatenate([v, sa], axis=0)
        kb_end = jnp.concatenate([k * e_end, bm * e_end], axis=0)
        s_new = s_big * jnp.exp(cl_last) + _dot_tn(vs, kb_end, precision=HI)
        s_ref[b] = jnp.where(bd_state, s_new, 0.0)

        mean = _dot(y, ones_bd, precision=HI) * (1.0 / RW_HD)
        yc = y - mean
        var = _dot(yc * yc, ones_bd, precision=HI) * (1.0 / RW_HD)
        yn = yc * lax.rsqrt(var + GN_EPS) * lng_ref[...] + lnb_ref[...]
        bonus = _dot(r * k * rk_ref[...], ones_bd, precision=HI) * v
        y_ref[b] = (yn + bonus) * g


def _rwkv(zr, shift0, s0, rp, l, bb, c):
    bsz, t, _ = zr.shape
    vec = lambda w: _layer_spec((1, w), l)
    return pl.pallas_call(
        functools.partial(_rwkv_kernel, bb=bb, c=c),
        out_shape=(jax.ShapeDtypeStruct((bsz, t, BW), F32), jax.ShapeDtypeStruct((bsz, BW, BW), F32)),
        grid=(bsz // bb, t // c),
        in_specs=[pl.BlockSpec((bb, c, RW_IN), lambda i, j: (i, j, 0)),
                  pl.BlockSpec((bb, 1, RW_IN), lambda i, j: (i, 0, 0)),
                  pl.BlockSpec((bb, BW, BW), lambda i, j: (i, 0, 0)),
                  vec(RW_IN), vec(BW), _layer_spec((128, BW), l), vec(BW), _layer_spec((128, BW), l),
                  _layer_spec((128, BW), l), vec(BW), vec(BW), vec(BW), vec(BW), vec(BW)],
        out_specs=(pl.BlockSpec((bb, c, BW), lambda i, j: (i, j, 0)),
                   pl.BlockSpec((bb, BW, BW), lambda i, j: (i, 0, 0))),
        scratch_shapes=[pltpu.VMEM((bb, 8 + c, RW_IN), F32)],
        compiler_params=_params(("parallel", "arbitrary")),
        name="rwkv7",
    )(zr, shift0, s0, rp["mu"], rp["w0"], rp["w2"], rp["a0"], rp["a2"], rp["g2"], rp["k_k"], rp["k_a"],
      rp["r_k"], rp["ln_g"], rp["ln_b"])


def _conv_kernel(u_ref, buf_ref, w_ref, cb_ref, g_ref, b_ref, y_ref, nb_ref, xp_ref, *, tt):
    j = pl.program_id(1)
    halo = CONV_K - 1
    top = 32 - halo

    @pl.when(j == 0)
    def _():
        xp_ref[:, top:32, :] = buf_ref[...]

    xp_ref[:, 32:32 + tt, :] = u_ref[...]
    acc = xp_ref[:, top:top + tt, :] * w_ref[0:1, :] + cb_ref[...]
    for k in range(1, CONV_K):
        acc = acc + xp_ref[:, top + k:top + k + tt, :] * w_ref[k:k + 1, :]
    yn = _layer_norm(acc, g_ref[...], b_ref[...])
    y_ref[...] = yn * _sigmoid(yn)
    new = xp_ref[:, top + tt:32 + tt, :]
    xp_ref[:, top:32, :] = new
    nb_ref[...] = new


def _conv(u, buf, cp, l, bb, tt):
    bsz, t, _ = u.shape
    halo = CONV_K - 1
    vec = lambda: _layer_spec((1, BW), l)
    return pl.pallas_call(
        functools.partial(_conv_kernel, tt=tt),
        out_shape=(jax.ShapeDtypeStruct((bsz, t, BW), F32), jax.ShapeDtypeStruct((bsz, halo, BW), F32)),
        grid=(bsz // bb, t // tt),
        in_specs=[pl.BlockSpec((bb, tt, BW), lambda i, j: (i, j, 0)),
                  pl.BlockSpec((bb, halo, BW), lambda i, j: (i, 0, 0)),
                  _layer_spec((CONV_K, BW), l), vec(), vec(), vec()],
        out_specs=(pl.BlockSpec((bb, tt, BW), lambda i, j: (i, j, 0)),
                   pl.BlockSpec((bb, halo, BW), lambda i, j: (i, 0, 0))),
        scratch_shapes=[pltpu.VMEM((bb, 32 + tt, BW), F32)],
        compiler_params=_params(("parallel", "arbitrary")),
        name="conv_module",
    )(u, buf, cp["w"], cp["b"], cp["ln_g"], cp["ln_b"])


def _gelu_tanh(x):
    return 0.5 * x * (1.0 + jnp.tanh(math.sqrt(2.0 / math.pi) * (x + 0.044715 * (x * x * x))))


def _s5_out(u, hre, him, cre_ref, cim_ref, d_ref, gw_ref, gb_ref):
    y = _dot(hre.astype(BF16), cre_ref[...]) - _dot(him.astype(BF16), cim_ref[...]) + d_ref[...] * u
    y = _gelu_tanh(y)
    return y * _sigmoid(_dot(y.astype(BF16), gw_ref[...]) + gb_ref[...])


def _s5_prompt_kernel(u_ref, bre_ref, bim_ref, pre_ref, pim_ref, cre_ref, cim_ref, d_ref, gw_ref, gb_ref,
                      y_ref, hre_ref, him_ref, *, tt):
    j = pl.program_id(1)

    @pl.when(j == 0)
    def _():
        hre_ref[...] = jnp.zeros_like(hre_ref)
        him_ref[...] = jnp.zeros_like(him_ref)

    u = u_ref[...]
    ub = u.astype(BF16)
    xre = _dot(ub, bre_ref[...])
    xim = _dot(ub, bim_ref[...])
    row = lax.broadcasted_iota(jnp.int32, (tt, S5_N), 0)
    lre = pre_ref[0:1, :]
    lim = pim_ref[0:1, :]
    cre = lre * hre_ref[...] - lim * him_ref[...]
    cim = lre * him_ref[...] + lim * hre_ref[...]
    xre = jnp.where(row == 0, xre + cre, xre)
    xim = jnp.where(row == 0, xim + cim, xim)
    for s in range(tt.bit_length() - 1):
        d = 1 << s
        are = pre_ref[s:s + 1, :]
        aim = pim_ref[s:s + 1, :]
        sre = pltpu.roll(xre, d, 0)
        sim = pltpu.roll(xim, d, 0)
        keep = row >= d
        xre, xim = (xre + jnp.where(keep, are * sre - aim * sim, 0.0),
                    xim + jnp.where(keep, are * sim + aim * sre, 0.0))
    hre_ref[...] = xre[tt - 1:tt, :]
    him_ref[...] = xim[tt - 1:tt, :]
    y_ref[...] = _s5_out(u, xre, xim, cre_ref, cim_ref, d_ref, gw_ref, gb_ref)


def _s5_prompt(u, sp, l, tt):
    bsz, t, _ = u.shape
    npow = sp["pre"].shape[1]
    return pl.pallas_call(
        functools.partial(_s5_prompt_kernel, tt=tt),
        out_shape=(jax.ShapeDtypeStruct((bsz, t, BW), F32),
                   jax.ShapeDtypeStruct((bsz, 1, S5_N), F32), jax.ShapeDtypeStruct((bsz, 1, S5_N), F32)),
        grid=(bsz, t // tt),
        in_specs=[pl.BlockSpec((None, tt, BW), lambda i, j: (i, j, 0)),
                  _layer_spec((BW, S5_N), l), _layer_spec((BW, S5_N), l),
                  _layer_spec((npow, S5_N), l), _layer_spec((npow, S5_N), l),
                  _layer_spec((S5_N, BW), l), _layer_spec((S5_N, BW), l),
                  _layer_spec((1, BW), l), _layer_spec((BW, BW), l), _layer_spec((1, BW), l)],
        out_specs=(pl.BlockSpec((None, tt, BW), lambda i, j: (i, j, 0)),
                   pl.BlockSpec((None, 1, S5_N), lambda i, j: (i, 0, 0)),
                   pl.BlockSpec((None, 1, S5_N), lambda i, j: (i, 0, 0))),
        compiler_params=_params(("parallel", "arbitrary")),
        name="s5_prompt",
    )(u, sp["bre"], sp["bim"], sp["pre"], sp["pim"], sp["cre"], sp["cim"], sp["d"], sp["glu_w"], sp["glu_b"])


def _s5_sample_kernel(u_ref, h0re_ref, h0im_ref, bre_ref, bim_ref, pre_ref, pim_ref, cre_ref, cim_ref,
                      d_ref, gw_ref, gb_ref, y_ref, hre_ref, him_ref, *, ts):
    lre = pre_ref[0:1, :]
    lim = pim_ref[0:1, :]
    hre = h0re_ref[...]
    him = h0im_ref[...]
    for t in range(ts):
        u = u_ref[t]
        ub = u.astype(BF16)
        hre, him = (lre * hre - lim * him + _dot(ub, bre_ref[...]),
                    lre * him + lim * hre + _dot(ub, bim_ref[...]))
        y_ref[t] = _s5_out(u, hre, him, cre_ref, cim_ref, d_ref, gw_ref, gb_ref)
    hre_ref[...] = hre
    him_ref[...] = him


def _s5_sample(u_tm, h0re, h0im, sp, l):
    ts, bsz, _ = u_tm.shape
    npow = sp["pre"].shape[1]
    return pl.pallas_call(
        functools.partial(_s5_sample_kernel, ts=ts),
        out_shape=(jax.ShapeDtypeStruct((ts, bsz, BW), F32),
                   jax.ShapeDtypeStruct((bsz, S5_N), F32), jax.ShapeDtypeStruct((bsz, S5_N), F32)),
        grid=(1,),
        in_specs=[_const_spec((ts, bsz, BW)), _const_spec((bsz, S5_N)), _const_spec((bsz, S5_N)),
                  _layer_spec((BW, S5_N), l), _layer_spec((BW, S5_N), l),
                  _layer_spec((npow, S5_N), l), _layer_spec((npow, S5_N), l),
                  _layer_spec((S5_N, BW), l), _layer_spec((S5_N, BW), l),
                  _layer_spec((1, BW), l), _layer_spec((BW, BW), l), _layer_spec((1, BW), l)],
        out_specs=(_const_spec((ts, bsz, BW)), _const_spec((bsz, S5_N)), _const_spec((bsz, S5_N))),
        compiler_params=_params(("arbitrary",)),
        name="s5_sample",
    )(u_tm, h0re, h0im, sp["bre"], sp["bim"], sp["pre"], sp["pim"], sp["cre"], sp["cim"], sp["d"],
      sp["glu_w"], sp["glu_b"])


def _merge_kernel(h_ref, ya_ref, yb_ref, yc_ref, yd_ref, wg_ref, bg_ref, wbr_ref, wo_ref, g_ref, b_ref, o_ref):
    h = h_ref[...]
    hb = h.astype(BF16)
    m = None
    for n, y_ref in enumerate((ya_ref, yb_ref, yc_ref, yd_ref)):
        gate = _sigmoid(_dot(hb, wg_ref[:, n * D_MODEL:(n + 1) * D_MODEL]) + bg_ref[n:n + 1, :])
        term = gate * _dot(y_ref[...].astype(BF16), wbr_ref[n])
        m = term if m is None else m + term
    out = _dot(m.astype(BF16), wo_ref[...])
    o_ref[...] = _layer_norm(ALPHA * h + out, g_ref[...], b_ref[...])


def _merge(h, ys, mw, l):
    n = h.shape[0]
    tm = _pick(n, (512, 256, 128, 64, 32, 16, 8))
    row = lambda w: pl.BlockSpec((tm, w), lambda i: (i, 0))
    return pl.pallas_call(
        _merge_kernel,
        out_shape=jax.ShapeDtypeStruct((n, D_MODEL), F32),
        grid=(n // tm,),
        in_specs=[row(D_MODEL), row(BW), row(BW), row(BW), row(BW),
                  _layer_spec((D_MODEL, N_BRANCH * D_MODEL), l), _layer_spec((N_BRANCH, D_MODEL), l),
                  _layer_spec((N_BRANCH, BW, D_MODEL), l), _layer_spec((D_MODEL, D_MODEL), l),
                  _layer_spec((1, D_MODEL), l), _layer_spec((1, D_MODEL), l)],
        out_specs=row(D_MODEL),
        compiler_params=_params(("parallel",)),
        name="branch_merge",
    )(h, *ys, mw["wg"], mw["bg"], mw["wbr"], mw["wo"], mw["g"], mw["b"])


def _pad_rows(w, lo, total):
    return jnp.pad(w, ((0, 0), (lo, total - lo - w.shape[1]), (0, 0)))


def _s5_params(a_re, a_im, log_dt, b_re, b_im, c_re, c_im, npow):
    dep = a_re.shape[0]
    lam_c = jnp.exp(lax.complex(a_re, a_im) * jnp.exp(log_dt)[..., None])
    b_bar = ((lam_c - 1.0) / lax.complex(a_re, a_im))[..., None] * lax.complex(b_re, b_im)
    eye = jnp.eye(S5_GROUPS, dtype=F32)
    def bmat(x):
        return jnp.einsum("lgpc,gk->lgckp", x, eye).reshape(dep, BW, S5_N)
    def cmat(x):
        return jnp.einsum("lgcp,gk->lgpkc", x, eye).reshape(dep, S5_N, BW)
    pows = [lam_c.reshape(dep, S5_N)]
    for _ in range(npow - 1):
        pows.append(pows[-1] * pows[-1])
    pw = jnp.stack(pows, axis=1)
    return dict(bre=bmat(jnp.real(b_bar)).astype(BF16), bim=bmat(jnp.imag(b_bar)).astype(BF16),
                pre=jnp.real(pw), pim=jnp.imag(pw),
                cre=cmat(c_re).astype(BF16), cim=cmat(c_im).astype(BF16))


def _rope_tables(pos):
    inv = 1.0 / (ROPE_BASE ** (jnp.arange(0, MLA_ROPE, 2, dtype=F32) / MLA_ROPE))
    ang = pos.astype(F32)[:, None] * inv[None, :]
    cos, sin = jnp.cos(ang), jnp.sin(ang)
    return jnp.concatenate([cos, cos], -1), jnp.concatenate([-sin, sin], -1)


def kernel(x_prompt, x_sample, cache_ckv, cache_kpe, state_rwkv, state_rwkv_shift, state_conv, state_s5_re, state_s5_im, page_table, ffn1_w_in, ffn1_w_down, ln1_g, ln1_b, w_in, b_gate, mla_q_norm, mla_w_uq, mla_kv_norm, mla_w_uk, mla_w_uv, rwkv_mu, rwkv_w0, rwkv_w2, rwkv_a0, rwkv_a2, rwkv_g2, rwkv_k_k, rwkv_k_a, rwkv_r_k, rwkv_ln_g, rwkv_ln_b, conv_w, conv_b, conv_ln_g, conv_ln_b, s5_a_re, s5_a_im, s5_log_dt, s5_b_re, s5_b_im, s5_c_re, s5_c_im, s5_d, s5_glu_w, s5_glu_b, w_branch, w_out, ln2_g, ln2_b, ffn2_w_in, ffn2_w_down, ln3_g, ln3_b):
    bp, tp, _ = x_prompt.shape
    bs, ts, _ = x_sample.shape
    depth = w_in.shape[0]
    n_p, n_s = bp * tp, bs * ts
    past = page_table.shape[1] * PAGE
    halo = CONV_K - 1
    vec = lambda w: w[:, None, :]

    o_kv = MLA_Q
    o_kp = o_kv + MLA_KV
    o_r = o_kp + MLA_ROPE
    o_c = o_r + RW_IN
    o_s = o_c + 2 * BW
    o_g = o_s + BW
    half = MLA_ROPE // 2
    wkp = w_in[:, :, o_kp:o_r]
    uq = mla_w_uq.reshape(depth, MLA_Q, MLA_H, MLA_NOPE + MLA_ROPE)
    uq_pe = uq[..., MLA_NOPE:]
    swap = lambda x: jnp.concatenate([x[..., half:], x[..., :half]], -1)
    to_heads = lambda x: jnp.moveaxis(x, 2, 1).astype(BF16)
    pw = dict(wq=w_in[:, :, :o_kv].astype(BF16), wkv=w_in[:, :, o_kv:o_kp].astype(BF16),
              wkp=wkp.astype(BF16), wkps=swap(wkp).astype(BF16),
              wr=w_in[:, :, o_r:o_c].astype(BF16), wc=w_in[:, :, o_c:o_s].astype(BF16),
              ws=w_in[:, :, o_s:o_g].astype(BF16),
              gq=vec(mla_q_norm), gkv=vec(mla_kv_norm),
              wn=to_heads(uq[..., :MLA_NOPE]), wp=to_heads(uq_pe), wps=to_heads(swap(uq_pe)),
              wuk=jnp.transpose(mla_w_uk, (0, 2, 3, 1)).astype(BF16))
    eye_h = jnp.eye(MLA_H, dtype=F32)
    wuv = jnp.einsum("lrhv,hk->lhrkv", mla_w_uv, eye_h).reshape(depth, MLA_H, MLA_KV, BW).astype(BF16)
    rp = dict(mu=vec(rwkv_mu), w0=vec(rwkv_w0), a0=vec(rwkv_a0),
              w2=_pad_rows(rwkv_w2, 0, 128).astype(BF16), a2=_pad_rows(rwkv_a2, 32, 128).astype(BF16),
              g2=_pad_rows(rwkv_g2, 64, 128).astype(BF16),
              k_k=vec(rwkv_k_k), k_a=vec(rwkv_k_a), r_k=vec(rwkv_r_k.reshape(depth, BW)),
              ln_g=vec(rwkv_ln_g), ln_b=vec(rwkv_ln_b))
    cp = dict(w=conv_w, b=vec(conv_b), ln_g=vec(conv_ln_g), ln_b=vec(conv_ln_b))
    tt_s5 = _pick(tp, (512, 256, 128))
    sp = _s5_params(s5_a_re, s5_a_im, s5_log_dt, s5_b_re, s5_b_im, s5_c_re, s5_c_im, max(1, tt_s5.bit_length() - 1))
    sp.update(d=vec(s5_d), glu_w=s5_glu_w.astype(BF16), glu_b=vec(s5_glu_b))
    mw = dict(wg=w_in[:, :, o_g:].astype(BF16), bg=b_gate, wbr=w_branch.astype(BF16), wo=w_out.astype(BF16),
              g=vec(ln2_g), b=vec(ln2_b))
    f1 = (ffn1_w_in.astype(BF16), ffn1_w_down.astype(BF16), vec(ln1_g), vec(ln1_b))
    f2 = (ffn2_w_in.astype(BF16), ffn2_w_down.astype(BF16), vec(ln3_g), vec(ln3_b))

    cos_p, sin_p = _rope_tables(jnp.arange(tp))
    cos_s, sin_s = _rope_tables(past + jnp.arange(ts))
    cos2 = jnp.concatenate([jnp.tile(cos_p, (bp, 1)), jnp.tile(cos_s, (bs, 1))], 0)
    sin2 = jnp.concatenate([jnp.tile(sin_p, (bp, 1)), jnp.tile(sin_s, (bs, 1))], 0)

    s_in = jnp.einsum("lbhij,hk->lbhikj", state_rwkv, jnp.eye(RW_H, dtype=F32)).reshape(depth, bs, BW, BW)
    c_rw = _pick(tp, (64, 32, 16, 8))
    bb_rw = _pick(bs, (4, 2, 1))
    tt_cv = _pick(tp, (512, 256, 128, 64, 32))
    bb_cv = _pick(bs, (32, 16, 8, 4, 2, 1))

    x = jnp.concatenate([x_prompt.reshape(n_p, D_MODEL), x_sample.reshape(n_s, D_MODEL)], 0)
    outs = [[] for _ in range(14)]
    for l in range(depth):
        h = _ffn_ln(x, *f1, l)
        qcat, ckv, kpe, kcat, zr, u, zs = _prep(h, cos2, sin2, pw, l)
        ya_p = _attn_prompt(qcat, kcat, wuv, l, bp, tp)
        ya_s = _attn_sample(page_table, qcat, ckv, kpe, cache_ckv, cache_kpe, wuv, l, n_p, bs, ts)
        zr_p = zr[:n_p].reshape(bp, tp, RW_IN)
        zr_s = zr[n_p:].reshape(bs, ts, RW_IN)
        yb_p, srw_p = _rwkv(zr_p, jnp.zeros((bp, 1, RW_IN), F32), jnp.zeros((bp, BW, BW), F32), rp, l, bp, c_rw)
        yb_s, srw_s = _rwkv(zr_s, state_rwkv_shift[l][:, None, :], s_in[l], rp, l, bb_rw, ts)
        yc_p, cv_p = _conv(u[:n_p].reshape(bp, tp, BW), jnp.zeros((bp, halo, BW), F32), cp, l, 1, tt_cv)
        yc_s, cv_s = _conv(u[n_p:].reshape(bs, ts, BW), state_conv[l], cp, l, bb_cv, ts)
        yd_p, s5re_p, s5im_p = _s5_prompt(zs[:n_p].reshape(bp, tp, BW), sp, l, tt_s5)
        yd_s, s5re_s, s5im_s = _s5_sample(jnp.swapaxes(zs[n_p:].reshape(bs, ts, BW), 0, 1),
                                          state_s5_re[l].reshape(bs, S5_N), state_s5_im[l].reshape(bs, S5_N), sp, l)
        yd_s = jnp.swapaxes(yd_s, 0, 1)
        cat = lambda a, b: jnp.concatenate([a.reshape(n_p, BW), b.reshape(n_s, BW)], 0)
        x = _merge(h, (cat(ya_p, ya_s), cat(yb_p, yb_s), cat(yc_p, yc_s), cat(yd_p, yd_s)), mw, l)
        x = _ffn_ln(x, *f2, l)

        diag = lambda s: jnp.stack([s[:, i * RW_HD:(i + 1) * RW_HD, i * RW_HD:(i + 1) * RW_HD]
                                    for i in range(RW_H)], 1)
        new_p = (ckv[:n_p].reshape(bp, tp, MLA_KV), kpe[:n_p].reshape(bp, tp, MLA_ROPE), diag(srw_p),
                 zr_p[:, -1], cv_p, s5re_p.reshape(bp, S5_GROUPS, S5_P), s5im_p.reshape(bp, S5_GROUPS, S5_P))
        new_s = (ckv[n_p:].reshape(bs, ts, MLA_KV), kpe[n_p:].reshape(bs, ts, MLA_ROPE), diag(srw_s),
                 zr_s[:, -1], cv_s, s5re_s.reshape(bs, S5_GROUPS, S5_P), s5im_s.reshape(bs, S5_GROUPS, S5_P))
        for i, a in enumerate(new_p + new_s):
            outs[i].append(a)
    return (x[:n_p].reshape(bp, tp, D_MODEL), x[n_p:].reshape(bs, ts, D_MODEL)) + tuple(jnp.stack(o) for o in outs)
```

```python
import functools
import math

import jax
import jax.numpy as jnp
from jax import lax
from jax.experimental import pallas as pl
from jax.experimental.pallas import tpu as pltpu

F32 = jnp.float32
BF16 = jnp.bfloat16
HI = lax.Precision.HIGHEST

D_MODEL = 1024
PAGE = 128
BW = D_MODEL // 4
N_BRANCH = 4
MLA_V = 64
MLA_H = BW // MLA_V
MLA_NOPE = 64
MLA_ROPE = 32
MLA_Q = D_MODEL // 4
MLA_KV = D_MODEL // 8
QK = MLA_KV + MLA_ROPE
KW = 256
ROPE_BASE = 10000.0
RW_HD = 64
RW_H = BW // RW_HD
RW_IN = 3 * BW + 128
CONV_K = 31
S5_G = 16
S5_GROUPS = BW // S5_G
S5_P = 64
S5_N = S5_GROUPS * S5_P
D_FF = 2816
FF_CHUNK = 256
DEPTH = 4
ALPHA = (2 * DEPTH) ** 0.25
ATTN_SCALE = (MLA_NOPE + MLA_ROPE) ** -0.5
LN_EPS = 1e-5
RMS_EPS = 1e-6
GN_EPS = 64e-5
VMEM_LIMIT = 56 * 1024 * 1024


def _pick(n, cands):
    for c in cands:
        if n % c == 0:
            return c
    raise ValueError(f"no tile in {cands} divides {n}")


def _const_spec(shape):
    nd = len(shape)
    return pl.BlockSpec(shape, lambda *_: (0,) * nd)


def _layer_spec(shape, l):
    nd = len(shape)
    return pl.BlockSpec((None,) + tuple(shape), lambda *_: (l,) + (0,) * nd)


def _params(sem, vmem=VMEM_LIMIT):
    return pltpu.CompilerParams(dimension_semantics=sem, vmem_limit_bytes=vmem)


def _dot(a, b, **kw):
    return jnp.dot(a, b, preferred_element_type=F32, **kw)


def _dot_nt(a, b, **kw):
    return lax.dot_general(a, b, (((1,), (1,)), ((), ())), preferred_element_type=F32, **kw)


def _dot_tn(a, b, **kw):
    return lax.dot_general(a, b, (((0,), (0,)), ((), ())), preferred_element_type=F32, **kw)


def _layer_norm(x, g, b):
    mu = jnp.mean(x, -1, keepdims=True)
    xc = x - mu
    var = jnp.mean(xc * xc, -1, keepdims=True)
    return xc * lax.rsqrt(var + LN_EPS) * g + b


def _rms_norm(x, g):
    return x * lax.rsqrt(jnp.mean(x * x, -1, keepdims=True) + RMS_EPS) * g


def _sigmoid(x):
    return 1.0 / (1.0 + jnp.exp(-x))


def _ffn_ln_kernel(x_ref, win_ref, wdn_ref, g_ref, b_ref, o_ref, acc_ref):
    x = x_ref[...]
    xb = x.astype(BF16)
    for c in range(D_FF // FF_CHUNK):
        lo = c * FF_CHUNK
        a = _dot(xb, win_ref[:, lo:lo + FF_CHUNK])
        b = _dot(xb, win_ref[:, D_FF + lo:D_FF + lo + FF_CHUNK])
        h = (a * _sigmoid(a) * b).astype(BF16)
        d = _dot(h, wdn_ref[lo:lo + FF_CHUNK, :])
        if c == 0:
            acc_ref[...] = d
        else:
            acc_ref[...] += d
    o_ref[...] = _layer_norm(ALPHA * x + 0.5 * acc_ref[...], g_ref[...], b_ref[...])


def _ffn_ln(x, w_in, w_down, g, b, l):
    n = x.shape[0]
    tm = _pick(n, (512, 256, 128, 64, 32, 16, 8))
    return pl.pallas_call(
        _ffn_ln_kernel,
        out_shape=jax.ShapeDtypeStruct((n, D_MODEL), F32),
        grid=(n // tm,),
        in_specs=[pl.BlockSpec((tm, D_MODEL), lambda i: (i, 0)),
                  _layer_spec((D_MODEL, 2 * D_FF), l), _layer_spec((D_FF, D_MODEL), l),
                  _layer_spec((1, D_MODEL), l), _layer_spec((1, D_MODEL), l)],
        out_specs=pl.BlockSpec((tm, D_MODEL), lambda i: (i, 0)),
        scratch_shapes=[pltpu.VMEM((tm, D_MODEL), F32)],
        compiler_params=_params(("parallel",)),
        name="ffn_ln",
    )(x, w_in, w_down, g, b)


def _prep_kernel(h_ref, cos_ref, sin_ref, wq_ref, wkv_ref, wkp_ref, wkps_ref, wr_ref, wc_ref, ws_ref,
                 gq_ref, gkv_ref, wn_ref, wp_ref, wps_ref, wuk_ref,
                 qcat_ref, q16_ref, ckv_ref, kpe_ref, kcat_ref, zr_ref, u_ref, zs_ref):
    hb = h_ref[...].astype(BF16)
    cos2 = cos_ref[...]
    sin2 = sin_ref[...]
    tm = hb.shape[0]
    zq = _rms_norm(_dot(hb, wq_ref[...]), gq_ref[...]).astype(BF16)
    for h in range(MLA_H):
        qn = _dot(zq, wn_ref[h]).astype(BF16)
        ql = _dot(qn, wuk_ref[h])
        qpe = _dot(zq, wp_ref[h]) * cos2 + _dot(zq, wps_ref[h]) * sin2
        qcat_ref[h, :, 0:MLA_KV] = ql
        qcat_ref[h, :, MLA_KV:QK] = qpe
        q16_ref[h, :, 0:MLA_KV] = ql.astype(BF16)
        q16_ref[h, :, MLA_KV:QK] = qpe.astype(BF16)
        q16_ref[h, :, QK:KW] = jnp.zeros((tm, KW - QK), BF16)
    ckv = _rms_norm(_dot(hb, wkv_ref[...]), gkv_ref[...])
    kpe = _dot(hb, wkp_ref[...]) * cos2 + _dot(hb, wkps_ref[...]) * sin2
    ckv_ref[...] = ckv
    kpe_ref[...] = kpe
    kcat_ref[:, 0:MLA_KV] = ckv.astype(BF16)
    kcat_ref[:, MLA_KV:QK] = kpe.astype(BF16)
    kcat_ref[:, QK:KW] = jnp.ones((tm, KW - QK), BF16)
    zr_ref[...] = _dot(hb, wr_ref[...])
    zc = _dot(hb, wc_ref[...])
    u_ref[...] = zc[:, :BW] * _sigmoid(zc[:, BW:])
    zs_ref[...] = _dot(hb, ws_ref[...])


def _prep(h, cos2, sin2, pw, l):
    n = h.shape[0]
    tm = _pick(n, (512, 256, 128, 64, 32, 16))
    row = lambda w: pl.BlockSpec((tm, w), lambda i: (i, 0))
    return pl.pallas_call(
        _prep_kernel,
        out_shape=(jax.ShapeDtypeStruct((MLA_H, n, QK), F32),
                   jax.ShapeDtypeStruct((MLA_H, n, KW), BF16),
                   jax.ShapeDtypeStruct((n, MLA_KV), F32),
                   jax.ShapeDtypeStruct((n, MLA_ROPE), F32),
                   jax.ShapeDtypeStruct((n, KW), BF16),
                   jax.ShapeDtypeStruct((n, RW_IN), F32),
                   jax.ShapeDtypeStruct((n, BW), F32),
                   jax.ShapeDtypeStruct((n, BW), F32)),
        grid=(n // tm,),
        in_specs=[row(D_MODEL), row(MLA_ROPE), row(MLA_ROPE),
                  _layer_spec((D_MODEL, MLA_Q), l), _layer_spec((D_MODEL, MLA_KV), l),
                  _layer_spec((D_MODEL, MLA_ROPE), l), _layer_spec((D_MODEL, MLA_ROPE), l),
                  _layer_spec((D_MODEL, RW_IN), l), _layer_spec((D_MODEL, 2 * BW), l),
                  _layer_spec((D_MODEL, BW), l),
                  _layer_spec((1, MLA_Q), l), _layer_spec((1, MLA_KV), l),
                  _layer_spec((MLA_H, MLA_Q, MLA_NOPE), l), _layer_spec((MLA_H, MLA_Q, MLA_ROPE), l),
                  _layer_spec((MLA_H, MLA_Q, MLA_ROPE), l), _layer_spec((MLA_H, MLA_NOPE, MLA_KV), l)],
        out_specs=(pl.BlockSpec((MLA_H, tm, QK), lambda i: (0, i, 0)),
                   pl.BlockSpec((MLA_H, tm, KW), lambda i: (0, i, 0)),
                   row(MLA_KV), row(MLA_ROPE), row(KW), row(RW_IN), row(BW), row(BW)),
        compiler_params=_params(("parallel",)),
        name="mixer_prep",
    )(h, cos2, sin2, pw["wq"], pw["wkv"], pw["wkp"], pw["wkps"], pw["wr"], pw["wc"], pw["ws"],
      pw["gq"], pw["gkv"], pw["wn"], pw["wp"], pw["wps"], pw["wuk"])


def _attn_prompt_kernel(q_ref, k_ref, wuv_ref, o_ref, m_ref, acc_ref, *, tq):
    i = pl.program_id(1)
    rep = lambda x: jnp.concatenate([x] * (tq // 128), axis=1)
    m_ref[...] = jnp.full(m_ref.shape, -jnp.inf, F32)
    acc_ref[...] = jnp.zeros(acc_ref.shape, F32)

    def update(kb, diagonal):
        k = k_ref[pl.ds(pl.multiple_of(kb * tq, tq), tq), :]
        for h in range(MLA_H):
            s = _dot_nt(q_ref[h], k) * ATTN_SCALE
            if diagonal:
                t = lax.broadcasted_iota(jnp.int32, (tq, tq), 0)
                c = lax.broadcasted_iota(jnp.int32, (tq, tq), 1)
                s = jnp.where(c <= t, s, -jnp.inf)
            m_prev = m_ref[h]
            m_new = jnp.maximum(m_prev, jnp.max(s, -1, keepdims=True))
            p = jnp.exp(s - rep(m_new))
            acc_ref[h] = rep(jnp.exp(m_prev - m_new)) * acc_ref[h] + _dot(p.astype(BF16), k)
            m_ref[h] = m_new

    def body(kb, carry):
        update(kb, False)
        return carry

    lax.fori_loop(0, i, body, 0)
    update(i, True)
    y = None
    for h in range(MLA_H):
        acc = acc_ref[h]
        o = (acc[:, :MLA_KV] / acc[:, KW - 1:KW]).astype(BF16)
        yh = _dot(o, wuv_ref[h])
        y = yh if y is None else y + yh
    o_ref[...] = y


def _attn_prompt(q16, kcat, wuv, l, bp, t):
    tq = KW
    nq = t // tq
    return pl.pallas_call(
        functools.partial(_attn_prompt_kernel, tq=tq),
        out_shape=jax.ShapeDtypeStruct((bp * t, BW), F32),
        grid=(bp, nq),
        in_specs=[pl.BlockSpec((MLA_H, tq, KW), lambda b, i: (0, b * nq + i, 0)),
                  pl.BlockSpec((t, KW), lambda b, i: (b, 0)),
                  _layer_spec((MLA_H, MLA_KV, BW), l)],
        out_specs=pl.BlockSpec((tq, BW), lambda b, i: (b * nq + i, 0)),
        scratch_shapes=[pltpu.VMEM((MLA_H, tq, 128), F32), pltpu.VMEM((MLA_H, tq, KW), F32)],
        compiler_params=_params(("parallel", "parallel")),
        name="mla_prompt",
    )(q16, kcat, wuv)


def _attn_sample_kernel(pt_ref, q_ref, cn_ref, pn_ref, ckv_hbm, kpe_hbm, wuv_ref, o_ref,
                        kbuf, pbuf, sem, *, l, ts, n_pages):
    b = pl.program_id(0)
    slot = b % 2

    def page_copies(bb, p, sl):
        pg = pt_ref[bb, p]
        dst = pl.ds(pl.multiple_of(p * PAGE, PAGE), PAGE)
        return (pltpu.make_async_copy(ckv_hbm.at[l, pg], kbuf.at[sl, dst, :], sem.at[0, sl]),
                pltpu.make_async_copy(kpe_hbm.at[l, pg], pbuf.at[sl, :, dst], sem.at[1, sl]))

    def start_fetch(bb, sl):
        def body(p, carry):
            for cp in page_copies(bb, p, sl):
                cp.start()
            return carry
        lax.fori_loop(0, n_pages, body, 0)

    @pl.when(b == 0)
    def _():
        start_fetch(0, 0)

    @pl.when(b + 1 < pl.num_programs(0))
    def _():
        start_fetch(b + 1, 1 - slot)

    def wait_body(p, carry):
        for cp in page_copies(b, p, slot):
            cp.wait()
        return carry
    lax.fori_loop(0, n_pages, wait_body, 0)

    rows = MLA_H * ts
    q = q_ref[...].reshape(rows, QK)
    ql = q[:, :MLA_KV].astype(BF16)
    qp = q[:, MLA_KV:].astype(BF16)
    kc = kbuf[slot].astype(BF16)
    kp = pbuf[slot].astype(BF16)
    cn = cn_ref[...].astype(BF16)
    pn = pn_ref[...].astype(BF16)
    s = (_dot_nt(ql, kc) + _dot(qp, kp)) * ATTN_SCALE
    sn = (_dot_nt(ql, cn) + _dot_nt(qp, pn)) * ATTN_SCALE
    t = lax.broadcasted_iota(jnp.int32, (rows, ts), 0) % ts
    c = lax.broadcasted_iota(jnp.int32, (rows, ts), 1)
    sn = jnp.where(c <= t, sn, -jnp.inf)
    m = jnp.maximum(jnp.max(s, -1, keepdims=True), jnp.max(sn, -1, keepdims=True))
    p = jnp.exp(s - m)
    pnew = jnp.exp(sn - m)
    den = jnp.sum(p, -1, keepdims=True) + jnp.sum(pnew, -1, keepdims=True)
    o = ((_dot(p.astype(BF16), kc) + _dot(pnew.astype(BF16), cn)) / den).astype(BF16)
    y = _dot(o[0:ts], wuv_ref[0])
    for h in range(1, MLA_H):
        y += _dot(o[h * ts:(h + 1) * ts], wuv_ref[h])
    o_ref[...] = y


def _attn_sample(page_table, qcat, ckv, kpe, cache_ckv, cache_kpe, wuv, l, row0, bs, ts):
    n_pages = page_table.shape[1]
    past = n_pages * PAGE
    blk0 = row0 // ts
    grid_spec = pltpu.PrefetchScalarGridSpec(
        num_scalar_prefetch=1,
        grid=(bs,),
        in_specs=[pl.BlockSpec((MLA_H, ts, QK), lambda b, pt: (0, blk0 + b, 0)),
                  pl.BlockSpec((ts, MLA_KV), lambda b, pt: (blk0 + b, 0)),
                  pl.BlockSpec((ts, MLA_ROPE), lambda b, pt: (blk0 + b, 0)),
                  pl.BlockSpec(memory_space=pl.ANY), pl.BlockSpec(memory_space=pl.ANY),
                  pl.BlockSpec((None, MLA_H, MLA_KV, BW), lambda b, pt: (l, 0, 0, 0))],
        out_specs=pl.BlockSpec((ts, BW), lambda b, pt: (b, 0)),
        scratch_shapes=[pltpu.VMEM((2, past, MLA_KV), F32), pltpu.VMEM((2, MLA_ROPE, past), F32),
                        pltpu.SemaphoreType.DMA((2, 2))])
    return pl.pallas_call(
        functools.partial(_attn_sample_kernel, l=l, ts=ts, n_pages=n_pages),
        out_shape=jax.ShapeDtypeStruct((bs * ts, BW), F32),
        grid_spec=grid_spec,
        compiler_params=_params(("arbitrary",)),
        name="mla_sample",
    )(page_table, qcat, ckv, kpe, cache_ckv, cache_kpe, wuv)


def _split3(x):
    hi = x.astype(BF16)
    r = x - hi.astype(F32)
    mid = r.astype(BF16)
    lo = (r - mid.astype(F32)).astype(BF16)
    return hi, mid, lo


def _dot_01x(m, x):
    hi, mid, lo = _split3(x)
    return _dot(m, hi) + _dot(m, mid) + _dot(m, lo)


def _dot_x01(x, m):
    hi, mid, lo = _split3(x)
    return _dot(hi, m) + _dot(mid, m) + _dot(lo, m)


def _bdot(a, b):
    return _dot(a.astype(BF16), b.astype(BF16))


def _rwkv_kernel(zr_ref, sh0_ref, s0_ref, mu_ref, w0_ref, w2_ref, a0_ref, a2_ref, g2_ref, kk_ref, ka_ref,
                 rk_ref, lng_ref, lnb_ref, y_ref, s_ref, zb_ref, *, bb, c):
    j = pl.program_id(1)

    @pl.when(j == 0)
    def _():
        s_ref[...] = s0_ref[...]
        zb_ref[:, 7:8, :] = sh0_ref[...]

    cw = RW_H * c
    bd_rows = (lax.broadcasted_iota(jnp.int32, (cw, BW), 0) // c
               == lax.broadcasted_iota(jnp.int32, (cw, BW), 1) // RW_HD)
    bd_sq = (lax.broadcasted_iota(jnp.int32, (cw, cw), 0) // c
             == lax.broadcasted_iota(jnp.int32, (cw, cw), 1) // c)
    bd_state = (lax.broadcasted_iota(jnp.int32, (BW, BW), 0) // RW_HD
                == lax.broadcasted_iota(jnp.int32, (BW, BW), 1) // RW_HD)
    ones_bd = jnp.where(bd_state, 1.0, 0.0).astype(BF16)
    tt = lax.broadcasted_iota(jnp.int32, (c, cw), 0)
    ss = lax.broadcasted_iota(jnp.int32, (c, cw), 1) % c
    strict = ss < tt
    incl = ss <= tt
    tri = jnp.where(lax.broadcasted_iota(jnp.int32, (c, c), 1)
                    <= lax.broadcasted_iota(jnp.int32, (c, c), 0), 1.0, 0.0).astype(BF16)

    def tile_bd(x, mask):
        return jnp.where(mask, jnp.concatenate([x] * RW_H, axis=0), 0.0).astype(BF16)

    for b in range(bb):
        z = zr_ref[b]
        zb_ref[b, 8:8 + c, :] = z
        prev = zb_ref[b, 7:7 + c, :]
        zb_ref[b, 7:8, :] = z[c - 1:c, :]
        zm = z + (prev - z) * mu_ref[...]
        r = zm[:, 0:BW]
        k = zm[:, BW:2 * BW]
        v = zm[:, 2 * BW:3 * BW]
        lo = zm[:, 3 * BW:]
        xw = -(w0_ref[...] + _dot(jnp.tanh(lo).astype(BF16), w2_ref[...]))
        softplus = jnp.maximum(xw, 0.0) + jnp.log(1.0 + jnp.exp(-jnp.abs(xw)))
        logw = -jnp.exp(-softplus - 0.5)
        a_sig = _sigmoid(a0_ref[...] + _dot(lo.astype(BF16), a2_ref[...]))
        g = _dot(_sigmoid(lo).astype(BF16), g2_ref[...])
        kk = k * kk_ref[...]
        kk = kk / jnp.maximum(jnp.sqrt(_dot_x01(kk * kk, ones_bd)), 1e-12)
        k = k * (1.0 + (a_sig - 1.0) * ka_ref[...])
        a = -kk
        bm = kk * a_sig

        cl = _dot_01x(tri, logw)
        cl_last = cl[c - 1:c, :]
        e_out = jnp.exp(-cl)
        e_end = jnp.exp(cl_last - cl)
        ar = jnp.concatenate([a * jnp.exp(cl - logw), r * jnp.exp(cl)], axis=0).astype(BF16)
        kb_bd = jnp.concatenate([tile_bd(k * e_out, bd_rows), tile_bd(bm * e_out, bd_rows)], axis=0)
        gm = _dot_nt(ar, kb_bd)
        l_ak = jnp.where(strict, gm[0:c, 0:cw], 0.0)
        l_ab = jnp.where(strict, gm[0:c, cw:2 * cw], 0.0)
        m_rk = jnp.where(incl, gm[c:2 * c, 0:cw], 0.0)
        m_rb = jnp.where(incl, gm[c:2 * c, cw:2 * cw], 0.0)
        lv = _bdot(jnp.concatenate([l_ak, m_rk], axis=0), tile_bd(v, bd_rows))
        p, q = l_ab, l_ab
        for _ in range(max(1, (c - 1).bit_length())):
            pq = _bdot(jnp.concatenate([p, q], axis=0), tile_bd(q, bd_sq))
            p = p + pq[0:c]
            q = pq[c:2 * c]
        kb_end = jnp.concatenate([k * e_end, bm * e_end], axis=0).astype(BF16)
        decay = jnp.exp(cl_last)

        s_big = s_ref[b]
        xs = _dot_nt(ar, s_big.astype(BF16))
        x0 = xs[0:c] + lv[0:c]
        sa = x0 + _bdot(p, tile_bd(x0, bd_rows))
        s_new = s_big * decay + _dot_tn(jnp.concatenate([v, sa], axis=0).astype(BF16), kb_end)
        s_ref[b] = jnp.where(bd_state, s_new, 0.0)
        y = xs[c:2 * c] + lv[c:2 * c] + _bdot(m_rb, tile_bd(sa, bd_rows))

        mean = _dot_x01(y, ones_bd) * (1.0 / RW_HD)
        yc = y - mean
        var = _dot_x01(yc * yc, ones_bd) * (1.0 / RW_HD)
        yn = yc * lax.rsqrt(var + GN_EPS) * lng_ref[...] + lnb_ref[...]
        bonus = _dot_x01(r * k * rk_ref[...], ones_bd) * v
        y_ref[b] = (yn + bonus) * g


def _rwkv(zr, shift0, s0, rp, l, bb, c):
    bsz, t, _ = zr.shape
    vec = lambda w: _layer_spec((1, w), l)
    return pl.pallas_call(
        functools.partial(_rwkv_kernel, bb=bb, c=c),
        out_shape=(jax.ShapeDtypeStruct((bsz, t, BW), F32), jax.ShapeDtypeStruct((bsz, BW, BW), F32)),
        grid=(bsz // bb, t // c),
        in_specs=[pl.BlockSpec((bb, c, RW_IN), lambda i, j: (i, j, 0)),
                  pl.BlockSpec((bb, 1, RW_IN), lambda i, j: (i, 0, 0)),
                  pl.BlockSpec((bb, BW, BW), lambda i, j: (i, 0, 0)),
                  vec(RW_IN), vec(BW), _layer_spec((128, BW), l), vec(BW), _layer_spec((128, BW), l),
                  _layer_spec((128, BW), l), vec(BW), vec(BW), vec(BW), vec(BW), vec(BW)],
        out_specs=(pl.BlockSpec((bb, c, BW), lambda i, j: (i, j, 0)),
                   pl.BlockSpec((bb, BW, BW), lambda i, j: (i, 0, 0))),
        scratch_shapes=[pltpu.VMEM((bb, 8 + c, RW_IN), F32)],
        compiler_params=_params(("parallel", "arbitrary")),
        name="rwkv7",
    )(zr, shift0, s0, rp["mu"], rp["w0"], rp["w2"], rp["a0"], rp["a2"], rp["g2"], rp["k_k"], rp["k_a"],
      rp["r_k"], rp["ln_g"], rp["ln_b"])


def _conv_kernel(u_ref, buf_ref, w_ref, cb_ref, g_ref, b_ref, y_ref, nb_ref, xp_ref, *, tt):
    j = pl.program_id(1)
    halo = CONV_K - 1
    top = 32 - halo

    @pl.when(j == 0)
    def _():
        xp_ref[:, top:32, :] = buf_ref[...]

    xp_ref[:, 32:32 + tt, :] = u_ref[...]
    acc = xp_ref[:, top:top + tt, :] * w_ref[0:1, :] + cb_ref[...]
    for k in range(1, CONV_K):
        acc = acc + xp_ref[:, top + k:top + k + tt, :] * w_ref[k:k + 1, :]
    yn = _layer_norm(acc, g_ref[...], b_ref[...])
    y_ref[...] = yn * _sigmoid(yn)
    new = xp_ref[:, top + tt:32 + tt, :]
    xp_ref[:, top:32, :] = new
    nb_ref[...] = new


def _conv(u, buf, cp, l, bb, tt):
    bsz, t, _ = u.shape
    halo = CONV_K - 1
    vec = lambda: _layer_spec((1, BW), l)
    return pl.pallas_call(
        functools.partial(_conv_kernel, tt=tt),
        out_shape=(jax.ShapeDtypeStruct((bsz, t, BW), F32), jax.ShapeDtypeStruct((bsz, halo, BW), F32)),
        grid=(bsz // bb, t // tt),
        in_specs=[pl.BlockSpec((bb, tt, BW), lambda i, j: (i, j, 0)),
                  pl.BlockSpec((bb, halo, BW), lambda i, j: (i, 0, 0)),
                  _layer_spec((CONV_K, BW), l), vec(), vec(), vec()],
        out_specs=(pl.BlockSpec((bb, tt, BW), lambda i, j: (i, j, 0)),
                   pl.BlockSpec((bb, halo, BW), lambda i, j: (i, 0, 0))),
        scratch_shapes=[pltpu.VMEM((bb, 32 + tt, BW), F32)],
        compiler_params=_params(("parallel", "arbitrary")),
        name="conv_module",
    )(u, buf, cp["w"], cp["b"], cp["ln_g"], cp["ln_b"])


def _gelu_tanh(x):
    return 0.5 * x * (1.0 + jnp.tanh(math.sqrt(2.0 / math.pi) * (x + 0.044715 * (x * x * x))))


def _s5_out(u, hre, him, cre_ref, cim_ref, d_ref, gw_ref, gb_ref):
    y = _dot(hre.astype(BF16), cre_ref[...]) - _dot(him.astype(BF16), cim_ref[...]) + d_ref[...] * u
    y = _gelu_tanh(y)
    return y * _sigmoid(_dot(y.astype(BF16), gw_ref[...]) + gb_ref[...])


def _s5_prompt_kernel(u_ref, bre_ref, bim_ref, pre_ref, pim_ref, cre_ref, cim_ref, d_ref, gw_ref, gb_ref,
                      y_ref, hre_ref, him_ref, *, tt):
    j = pl.program_id(1)

    @pl.when(j == 0)
    def _():
        hre_ref[...] = jnp.zeros_like(hre_ref)
        him_ref[...] = jnp.zeros_like(him_ref)

    u = u_ref[...]
    ub = u.astype(BF16)
    xre = _dot(ub, bre_ref[...])
    xim = _dot(ub, bim_ref[...])
    row = lax.broadcasted_iota(jnp.int32, (tt, S5_N), 0)
    lre = pre_ref[0:1, :]
    lim = pim_ref[0:1, :]
    cre = lre * hre_ref[...] - lim * him_ref[...]
    cim = lre * him_ref[...] + lim * hre_ref[...]
    xre = jnp.where(row == 0, xre + cre, xre)
    xim = jnp.where(row == 0, xim + cim, xim)
    for s in range(tt.bit_length() - 1):
        d = 1 << s
        are = pre_ref[s:s + 1, :]
        aim = pim_ref[s:s + 1, :]
        sre = pltpu.roll(xre, d, 0)
        sim = pltpu.roll(xim, d, 0)
        keep = row >= d
        xre, xim = (xre + jnp.where(keep, are * sre - aim * sim, 0.0),
                    xim + jnp.where(keep, are * sim + aim * sre, 0.0))
    hre_ref[...] = xre[tt - 1:tt, :]
    him_ref[...] = xim[tt - 1:tt, :]
    y_ref[...] = _s5_out(u, xre, xim, cre_ref, cim_ref, d_ref, gw_ref, gb_ref)


def _s5_prompt(u, sp, l, tt):
    bsz, t, _ = u.shape
    npow = sp["pre"].shape[1]
    return pl.pallas_call(
        functools.partial(_s5_prompt_kernel, tt=tt),
        out_shape=(jax.ShapeDtypeStruct((bsz, t, BW), F32),
                   jax.ShapeDtypeStruct((bsz, 1, S5_N), F32), jax.ShapeDtypeStruct((bsz, 1, S5_N), F32)),
        grid=(bsz, t // tt),
        in_specs=[pl.BlockSpec((None, tt, BW), lambda i, j: (i, j, 0)),
                  _layer_spec((BW, S5_N), l), _layer_spec((BW, S5_N), l),
                  _layer_spec((npow, S5_N), l), _layer_spec((npow, S5_N), l),
                  _layer_spec((S5_N, BW), l), _layer_spec((S5_N, BW), l),
                  _layer_spec((1, BW), l), _layer_spec((BW, BW), l), _layer_spec((1, BW), l)],
        out_specs=(pl.BlockSpec((None, tt, BW), lambda i, j: (i, j, 0)),
                   pl.BlockSpec((None, 1, S5_N), lambda i, j: (i, 0, 0)),
                   pl.BlockSpec((None, 1, S5_N), lambda i, j: (i, 0, 0))),
        compiler_params=_params(("parallel", "arbitrary")),
        name="s5_prompt",
    )(u, sp["bre"], sp["bim"], sp["pre"], sp["pim"], sp["cre"], sp["cim"], sp["d"], sp["glu_w"], sp["glu_b"])


def _s5_sample_kernel(u_ref, h0re_ref, h0im_ref, bre_ref, bim_ref, pre_ref, pim_ref, cre_ref, cim_ref,
                      d_ref, gw_ref, gb_ref, y_ref, hre_ref, him_ref, *, ts):
    lre = pre_ref[0:1, :]
    lim = pim_ref[0:1, :]
    hre = h0re_ref[...]
    him = h0im_ref[...]
    for t in range(ts):
        u = u_ref[t]
        ub = u.astype(BF16)
        hre, him = (lre * hre - lim * him + _dot(ub, bre_ref[...]),
                    lre * him + lim * hre + _dot(ub, bim_ref[...]))
        y_ref[t] = _s5_out(u, hre, him, cre_ref, cim_ref, d_ref, gw_ref, gb_ref)
    hre_ref[...] = hre
    him_ref[...] = him


def _s5_sample(u_tm, h0re, h0im, sp, l):
    ts, bsz, _ = u_tm.shape
    npow = sp["pre"].shape[1]
    return pl.pallas_call(
        functools.partial(_s5_sample_kernel, ts=ts),
        out_shape=(jax.ShapeDtypeStruct((ts, bsz, BW), F32),
                   jax.ShapeDtypeStruct((bsz, S5_N), F32), jax.ShapeDtypeStruct((bsz, S5_N), F32)),
        grid=(1,),
        in_specs=[_const_spec((ts, bsz, BW)), _const_spec((bsz, S5_N)), _const_spec((bsz, S5_N)),
                  _layer_spec((BW, S5_N), l), _layer_spec((BW, S5_N), l),
                  _layer_spec((npow, S5_N), l), _layer_spec((npow, S5_N), l),
                  _layer_spec((S5_N, BW), l), _layer_spec((S5_N, BW), l),
                  _layer_spec((1, BW), l), _layer_spec((BW, BW), l), _layer_spec((1, BW), l)],
        out_specs=(_const_spec((ts, bsz, BW)), _const_spec((bsz, S5_N)), _const_spec((bsz, S5_N))),
        compiler_params=_params(("arbitrary",)),
        name="s5_sample",
    )(u_tm, h0re, h0im, sp["bre"], sp["bim"], sp["pre"], sp["pim"], sp["cre"], sp["cim"], sp["d"],
      sp["glu_w"], sp["glu_b"])


def _merge_kernel(h_ref, ya_ref, yb_ref, yc_ref, yd_ref, wg_ref, bg_ref, wbr_ref, wo_ref, g_ref, b_ref, o_ref):
    h = h_ref[...]
    hb = h.astype(BF16)
    m = None
    for n, y_ref in enumerate((ya_ref, yb_ref, yc_ref, yd_ref)):
        gate = _sigmoid(_dot(hb, wg_ref[:, n * D_MODEL:(n + 1) * D_MODEL]) + bg_ref[n:n + 1, :])
        term = gate * _dot(y_ref[...].astype(BF16), wbr_ref[n])
        m = term if m is None else m + term
    out = _dot(m.astype(BF16), wo_ref[...])
    o_ref[...] = _layer_norm(ALPHA * h + out, g_ref[...], b_ref[...])


def _merge(h, ys, mw, l):
    n = h.shape[0]
    tm = _pick(n, (512, 256, 128, 64, 32, 16, 8))
    row = lambda w: pl.BlockSpec((tm, w), lambda i: (i, 0))
    return pl.pallas_call(
        _merge_kernel,
        out_shape=jax.ShapeDtypeStruct((n, D_MODEL), F32),
        grid=(n // tm,),
        in_specs=[row(D_MODEL), row(BW), row(BW), row(BW), row(BW),
                  _layer_spec((D_MODEL, N_BRANCH * D_MODEL), l), _layer_spec((N_BRANCH, D_MODEL), l),
                  _layer_spec((N_BRANCH, BW, D_MODEL), l), _layer_spec((D_MODEL, D_MODEL), l),
                  _layer_spec((1, D_MODEL), l), _layer_spec((1, D_MODEL), l)],
        out_specs=row(D_MODEL),
        compiler_params=_params(("parallel",)),
        name="branch_merge",
    )(h, *ys, mw["wg"], mw["bg"], mw["wbr"], mw["wo"], mw["g"], mw["b"])


def _pad_rows(w, lo, total):
    return jnp.pad(w, ((0, 0), (lo, total - lo - w.shape[1]), (0, 0)))


def _s5_params(a_re, a_im, log_dt, b_re, b_im, c_re, c_im, npow):
    dep = a_re.shape[0]
    dt = jnp.exp(log_dt)[..., None]
    mag = jnp.exp(a_re * dt)
    lam_re, lam_im = mag * jnp.cos(a_im * dt), mag * jnp.sin(a_im * dt)
    den = a_re * a_re + a_im * a_im
    co_re = ((lam_re - 1.0) * a_re + lam_im * a_im) / den
    co_im = (lam_im * a_re - (lam_re - 1.0) * a_im) / den
    bb_re = co_re[..., None] * b_re - co_im[..., None] * b_im
    bb_im = co_re[..., None] * b_im + co_im[..., None] * b_re
    eye = jnp.eye(S5_GROUPS, dtype=F32)
    def bmat(x):
        return jnp.einsum("lgpc,gk->lgckp", x, eye).reshape(dep, BW, S5_N)
    def cmat(x):
        return jnp.einsum("lgcp,gk->lgpkc", x, eye).reshape(dep, S5_N, BW)
    pre, pim = [lam_re.reshape(dep, S5_N)], [lam_im.reshape(dep, S5_N)]
    for _ in range(npow - 1):
        xr, xi = pre[-1], pim[-1]
        pre.append(xr * xr - xi * xi)
        pim.append(2.0 * xr * xi)
    return dict(bre=bmat(bb_re).astype(BF16), bim=bmat(bb_im).astype(BF16),
                pre=jnp.stack(pre, axis=1), pim=jnp.stack(pim, axis=1),
                cre=cmat(c_re).astype(BF16), cim=cmat(c_im).astype(BF16))


def _rope_tables(pos):
    inv = 1.0 / (ROPE_BASE ** (jnp.arange(0, MLA_ROPE, 2, dtype=F32) / MLA_ROPE))
    ang = pos.astype(F32)[:, None] * inv[None, :]
    cos, sin = jnp.cos(ang), jnp.sin(ang)
    return jnp.concatenate([cos, cos], -1), jnp.concatenate([-sin, sin], -1)


def kernel(x_prompt, x_sample, cache_ckv, cache_kpe, state_rwkv, state_rwkv_shift, state_conv, state_s5_re, state_s5_im, page_table, ffn1_w_in, ffn1_w_down, ln1_g, ln1_b, w_in, b_gate, mla_q_norm, mla_w_uq, mla_kv_norm, mla_w_uk, mla_w_uv, rwkv_mu, rwkv_w0, rwkv_w2, rwkv_a0, rwkv_a2, rwkv_g2, rwkv_k_k, rwkv_k_a, rwkv_r_k, rwkv_ln_g, rwkv_ln_b, conv_w, conv_b, conv_ln_g, conv_ln_b, s5_a_re, s5_a_im, s5_log_dt, s5_b_re, s5_b_im, s5_c_re, s5_c_im, s5_d, s5_glu_w, s5_glu_b, w_branch, w_out, ln2_g, ln2_b, ffn2_w_in, ffn2_w_down, ln3_g, ln3_b):
    bp, tp, _ = x_prompt.shape
    bs, ts, _ = x_sample.shape
    depth = w_in.shape[0]
    n_p, n_s = bp * tp, bs * ts
    past = page_table.shape[1] * PAGE
    halo = CONV_K - 1
    vec = lambda w: w[:, None, :]

    o_kv = MLA_Q
    o_kp = o_kv + MLA_KV
    o_r = o_kp + MLA_ROPE
    o_c = o_r + RW_IN
    o_s = o_c + 2 * BW
    o_g = o_s + BW
    half = MLA_ROPE // 2
    wkp = w_in[:, :, o_kp:o_r]
    uq = mla_w_uq.reshape(depth, MLA_Q, MLA_H, MLA_NOPE + MLA_ROPE)
    uq_pe = uq[..., MLA_NOPE:]
    swap = lambda x: jnp.concatenate([x[..., half:], x[..., :half]], -1)
    to_heads = lambda x: jnp.moveaxis(x, 2, 1).astype(BF16)
    pw = dict(wq=w_in[:, :, :o_kv].astype(BF16), wkv=w_in[:, :, o_kv:o_kp].astype(BF16),
              wkp=wkp.astype(BF16), wkps=swap(wkp).astype(BF16),
              wr=w_in[:, :, o_r:o_c].astype(BF16), wc=w_in[:, :, o_c:o_s].astype(BF16),
              ws=w_in[:, :, o_s:o_g].astype(BF16),
              gq=vec(mla_q_norm), gkv=vec(mla_kv_norm),
              wn=to_heads(uq[..., :MLA_NOPE]), wp=to_heads(uq_pe), wps=to_heads(swap(uq_pe)),
              wuk=jnp.transpose(mla_w_uk, (0, 2, 3, 1)).astype(BF16))
    eye_h = jnp.eye(MLA_H, dtype=F32)
    wuv = jnp.einsum("lrhv,hk->lhrkv", mla_w_uv, eye_h).reshape(depth, MLA_H, MLA_KV, BW).astype(BF16)
    rp = dict(mu=vec(rwkv_mu), w0=vec(rwkv_w0), a0=vec(rwkv_a0),
              w2=_pad_rows(rwkv_w2, 0, 128).astype(BF16), a2=_pad_rows(rwkv_a2, 32, 128).astype(BF16),
              g2=_pad_rows(rwkv_g2, 64, 128).astype(BF16),
              k_k=vec(rwkv_k_k), k_a=vec(rwkv_k_a), r_k=vec(rwkv_r_k.reshape(depth, BW)),
              ln_g=vec(rwkv_ln_g), ln_b=vec(rwkv_ln_b))
    cp = dict(w=conv_w, b=vec(conv_b), ln_g=vec(conv_ln_g), ln_b=vec(conv_ln_b))
    tt_s5 = _pick(tp, (512, 256, 128))
    sp = _s5_params(s5_a_re, s5_a_im, s5_log_dt, s5_b_re, s5_b_im, s5_c_re, s5_c_im, max(1, tt_s5.bit_length() - 1))
    sp.update(d=vec(s5_d), glu_w=s5_glu_w.astype(BF16), glu_b=vec(s5_glu_b))
    mw = dict(wg=w_in[:, :, o_g:].astype(BF16), bg=b_gate, wbr=w_branch.astype(BF16), wo=w_out.astype(BF16),
              g=vec(ln2_g), b=vec(ln2_b))
    f1 = (ffn1_w_in.astype(BF16), ffn1_w_down.astype(BF16), vec(ln1_g), vec(ln1_b))
    f2 = (ffn2_w_in.astype(BF16), ffn2_w_down.astype(BF16), vec(ln3_g), vec(ln3_b))

    cos_p, sin_p = _rope_tables(jnp.arange(tp))
    cos_s, sin_s = _rope_tables(past + jnp.arange(ts))
    cos2 = jnp.concatenate([jnp.tile(cos_p, (bp, 1)), jnp.tile(cos_s, (bs, 1))], 0)
    sin2 = jnp.concatenate([jnp.tile(sin_p, (bp, 1)), jnp.tile(sin_s, (bs, 1))], 0)

    s_in = jnp.einsum("lbhij,hk->lbhikj", state_rwkv, jnp.eye(RW_H, dtype=F32)).reshape(depth, bs, BW, BW)
    c_rw = _pick(tp, (64, 32, 16, 8))
    bb_rw = _pick(bs, (4, 2, 1))
    tt_cv = _pick(tp, (512, 256, 128, 64, 32))
    bb_cv = _pick(bs, (32, 16, 8, 4, 2, 1))

    cache_kpe_t = jnp.swapaxes(cache_kpe, 2, 3)
    x = jnp.concatenate([x_prompt.reshape(n_p, D_MODEL), x_sample.reshape(n_s, D_MODEL)], 0)
    outs = [[] for _ in range(14)]
    for l in range(depth):
        h = _ffn_ln(x, *f1, l)
        qcat, q16, ckv, kpe, kcat, zr, u, zs = _prep(h, cos2, sin2, pw, l)
        ya_p = _attn_prompt(q16, kcat, wuv, l, bp, tp)
        ya_s = _attn_sample(page_table, qcat, ckv, kpe, cache_ckv, cache_kpe_t, wuv, l, n_p, bs, ts)
        zr_p = zr[:n_p].reshape(bp, tp, RW_IN)
        zr_s = zr[n_p:].reshape(bs, ts, RW_IN)
        yb_p, srw_p = _rwkv(zr_p, jnp.zeros((bp, 1, RW_IN), F32), jnp.zeros((bp, BW, BW), F32), rp, l, bp, c_rw)
        yb_s, srw_s = _rwkv(zr_s, state_rwkv_shift[l][:, None, :], s_in[l], rp, l, bb_rw, ts)
        yc_p, cv_p = _conv(u[:n_p].reshape(bp, tp, BW), jnp.zeros((bp, halo, BW), F32), cp, l, 1, tt_cv)
        yc_s, cv_s = _conv(u[n_p:].reshape(bs, ts, BW), state_conv[l], cp, l, bb_cv, ts)
        yd_p, s5re_p, s5im_p = _s5_prompt(zs[:n_p].reshape(bp, tp, BW), sp, l, tt_s5)
        yd_s, s5re_s, s5im_s = _s5_sample(jnp.swapaxes(zs[n_p:].reshape(bs, ts, BW), 0, 1),
                                          state_s5_re[l].reshape(bs, S5_N), state_s5_im[l].reshape(bs, S5_N), sp, l)
        yd_s = jnp.swapaxes(yd_s, 0, 1)
        cat = lambda a, b: jnp.concatenate([a.reshape(n_p, BW), b.reshape(n_s, BW)], 0)
        x = _merge(h, (cat(ya_p, ya_s), cat(yb_p, yb_s), cat(yc_p, yc_s), cat(yd_p, yd_s)), mw, l)
        x = _ffn_ln(x, *f2, l)

        diag = lambda s: jnp.stack([s[:, i * RW_HD:(i + 1) * RW_HD, i * RW_HD:(i + 1) * RW_HD]
                                    for i in range(RW_H)], 1)
        new_p = (ckv[:n_p].reshape(bp, tp, MLA_KV), kpe[:n_p].reshape(bp, tp, MLA_ROPE), diag(srw_p),
                 zr_p[:, -1], cv_p, s5re_p.reshape(bp, S5_GROUPS, S5_P), s5im_p.reshape(bp, S5_GROUPS, S5_P))
        new_s = (ckv[n_p:].reshape(bs, ts, MLA_KV), kpe[n_p:].reshape(bs, ts, MLA_ROPE), diag(srw_s),
                 zr_s[:, -1], cv_s, s5re_s.reshape(bs, S5_GROUPS, S5_P), s5im_s.reshape(bs, S5_GROUPS, S5_P))
        for i, a in enumerate(new_p + new_s):
            outs[i].append(a)
    return (x[:n_p].reshape(bp, tp, D_MODEL), x[n_p:].reshape(bs, ts, D_MODEL)) + tuple(jnp.stack(o) for o in outs)
```

```python
import functools
import math

import jax
import jax.numpy as jnp
from jax import lax
from jax.experimental import pallas as pl
from jax.experimental.pallas import tpu as pltpu

F32 = jnp.float32
BF16 = jnp.bfloat16
HI = lax.Precision.HIGHEST

D_MODEL = 1024
PAGE = 128
BW = D_MODEL // 4
N_BRANCH = 4
MLA_V = 64
MLA_H = BW // MLA_V
MLA_NOPE = 64
MLA_ROPE = 32
MLA_Q = D_MODEL // 4
MLA_KV = D_MODEL // 8
QK = MLA_KV + MLA_ROPE
KW = 256
ROPE_BASE = 10000.0
RW_HD = 64
RW_H = BW // RW_HD
RW_IN = 3 * BW + 128
CONV_K = 31
S5_G = 16
S5_GROUPS = BW // S5_G
S5_P = 64
S5_N = S5_GROUPS * S5_P
D_FF = 2816
FF_CHUNK = 256
DEPTH = 4
ALPHA = (2 * DEPTH) ** 0.25
ATTN_SCALE = (MLA_NOPE + MLA_ROPE) ** -0.5
LN_EPS = 1e-5
RMS_EPS = 1e-6
GN_EPS = 64e-5
VMEM_LIMIT = 56 * 1024 * 1024


def _pick(n, cands):
    for c in cands:
        if n % c == 0:
            return c
    raise ValueError(f"no tile in {cands} divides {n}")


def _const_spec(shape):
    nd = len(shape)
    return pl.BlockSpec(shape, lambda *_: (0,) * nd)


def _layer_spec(shape, l):
    nd = len(shape)
    return pl.BlockSpec((None,) + tuple(shape), lambda *_: (l,) + (0,) * nd)


def _params(sem, vmem=VMEM_LIMIT):
    return pltpu.CompilerParams(dimension_semantics=sem, vmem_limit_bytes=vmem)


def _dot(a, b, **kw):
    return jnp.dot(a, b, preferred_element_type=F32, **kw)


def _dot_nt(a, b, **kw):
    return lax.dot_general(a, b, (((1,), (1,)), ((), ())), preferred_element_type=F32, **kw)


def _dot_tn(a, b, **kw):
    return lax.dot_general(a, b, (((0,), (0,)), ((), ())), preferred_element_type=F32, **kw)


def _layer_norm(x, g, b):
    mu = jnp.mean(x, -1, keepdims=True)
    xc = x - mu
    var = jnp.mean(xc * xc, -1, keepdims=True)
    return xc * lax.rsqrt(var + LN_EPS) * g + b


def _rms_norm(x, g):
    return x * lax.rsqrt(jnp.mean(x * x, -1, keepdims=True) + RMS_EPS) * g


def _sigmoid(x):
    return 1.0 / (1.0 + jnp.exp(-x))


def _ffn_ln_kernel(x_ref, win_ref, wdn_ref, g_ref, b_ref, o_ref, acc_ref):
    x = x_ref[...]
    xb = x.astype(BF16)
    for c in range(D_FF // FF_CHUNK):
        lo = c * FF_CHUNK
        a = _dot(xb, win_ref[:, lo:lo + FF_CHUNK])
        b = _dot(xb, win_ref[:, D_FF + lo:D_FF + lo + FF_CHUNK])
        h = (a * _sigmoid(a) * b).astype(BF16)
        d = _dot(h, wdn_ref[lo:lo + FF_CHUNK, :])
        if c == 0:
            acc_ref[...] = d
        else:
            acc_ref[...] += d
    o_ref[...] = _layer_norm(ALPHA * x + 0.5 * acc_ref[...], g_ref[...], b_ref[...])


def _ffn_ln(x, w_in, w_down, g, b, l):
    n = x.shape[0]
    tm = _pick(n, (512, 256, 128, 64, 32, 16, 8))
    return pl.pallas_call(
        _ffn_ln_kernel,
        out_shape=jax.ShapeDtypeStruct((n, D_MODEL), F32),
        grid=(n // tm,),
        in_specs=[pl.BlockSpec((tm, D_MODEL), lambda i: (i, 0)),
                  _layer_spec((D_MODEL, 2 * D_FF), l), _layer_spec((D_FF, D_MODEL), l),
                  _layer_spec((1, D_MODEL), l), _layer_spec((1, D_MODEL), l)],
        out_specs=pl.BlockSpec((tm, D_MODEL), lambda i: (i, 0)),
        scratch_shapes=[pltpu.VMEM((tm, D_MODEL), F32)],
        compiler_params=_params(("parallel",)),
        name="ffn_ln",
    )(x, w_in, w_down, g, b)


def _prep_kernel(h_ref, cos_ref, sin_ref, wq_ref, wkv_ref, wkp_ref, wkps_ref, wr_ref, wc_ref, ws_ref,
                 gq_ref, gkv_ref, wn_ref, wp_ref, wps_ref, wuk_ref,
                 qcat_ref, q16_ref, ckv_ref, kpe_ref, kcat_ref, zr_ref, u_ref, zs_ref):
    hb = h_ref[...].astype(BF16)
    cos2 = cos_ref[...]
    sin2 = sin_ref[...]
    tm = hb.shape[0]
    zq = _rms_norm(_dot(hb, wq_ref[...]), gq_ref[...]).astype(BF16)
    for h in range(MLA_H):
        qn = _dot(zq, wn_ref[h]).astype(BF16)
        ql = _dot(qn, wuk_ref[h])
        qpe = _dot(zq, wp_ref[h]) * cos2 + _dot(zq, wps_ref[h]) * sin2
        qcat_ref[h, :, 0:MLA_KV] = ql
        qcat_ref[h, :, MLA_KV:QK] = qpe
        q16_ref[h, :, 0:MLA_KV] = ql.astype(BF16)
        q16_ref[h, :, MLA_KV:QK] = qpe.astype(BF16)
        q16_ref[h, :, QK:KW] = jnp.zeros((tm, KW - QK), BF16)
    ckv = _rms_norm(_dot(hb, wkv_ref[...]), gkv_ref[...])
    kpe = _dot(hb, wkp_ref[...]) * cos2 + _dot(hb, wkps_ref[...]) * sin2
    ckv_ref[...] = ckv
    kpe_ref[...] = kpe
    kcat_ref[:, 0:MLA_KV] = ckv.astype(BF16)
    kcat_ref[:, MLA_KV:QK] = kpe.astype(BF16)
    kcat_ref[:, QK:KW] = jnp.ones((tm, KW - QK), BF16)
    zr_ref[...] = _dot(hb, wr_ref[...])
    zc = _dot(hb, wc_ref[...])
    u_ref[...] = zc[:, :BW] * _sigmoid(zc[:, BW:])
    zs_ref[...] = _dot(hb, ws_ref[...])


def _prep(h, cos2, sin2, pw, l):
    n = h.shape[0]
    tm = _pick(n, (512, 256, 128, 64, 32, 16))
    row = lambda w: pl.BlockSpec((tm, w), lambda i: (i, 0))
    return pl.pallas_call(
        _prep_kernel,
        out_shape=(jax.ShapeDtypeStruct((MLA_H, n, QK), F32),
                   jax.ShapeDtypeStruct((MLA_H, n, KW), BF16),
                   jax.ShapeDtypeStruct((n, MLA_KV), F32),
                   jax.ShapeDtypeStruct((n, MLA_ROPE), F32),
                   jax.ShapeDtypeStruct((n, KW), BF16),
                   jax.ShapeDtypeStruct((n, RW_IN), F32),
                   jax.ShapeDtypeStruct((n, BW), F32),
                   jax.ShapeDtypeStruct((n, BW), F32)),
        grid=(n // tm,),
        in_specs=[row(D_MODEL), row(MLA_ROPE), row(MLA_ROPE),
                  _layer_spec((D_MODEL, MLA_Q), l), _layer_spec((D_MODEL, MLA_KV), l),
                  _layer_spec((D_MODEL, MLA_ROPE), l), _layer_spec((D_MODEL, MLA_ROPE), l),
                  _layer_spec((D_MODEL, RW_IN), l), _layer_spec((D_MODEL, 2 * BW), l),
                  _layer_spec((D_MODEL, BW), l),
                  _layer_spec((1, MLA_Q), l), _layer_spec((1, MLA_KV), l),
                  _layer_spec((MLA_H, MLA_Q, MLA_NOPE), l), _layer_spec((MLA_H, MLA_Q, MLA_ROPE), l),
                  _layer_spec((MLA_H, MLA_Q, MLA_ROPE), l), _layer_spec((MLA_H, MLA_NOPE, MLA_KV), l)],
        out_specs=(pl.BlockSpec((MLA_H, tm, QK), lambda i: (0, i, 0)),
                   pl.BlockSpec((MLA_H, tm, KW), lambda i: (0, i, 0)),
                   row(MLA_KV), row(MLA_ROPE), row(KW), row(RW_IN), row(BW), row(BW)),
        compiler_params=_params(("parallel",)),
        name="mixer_prep",
    )(h, cos2, sin2, pw["wq"], pw["wkv"], pw["wkp"], pw["wkps"], pw["wr"], pw["wc"], pw["ws"],
      pw["gq"], pw["gkv"], pw["wn"], pw["wp"], pw["wps"], pw["wuk"])


def _attn_prompt_kernel(q_ref, k_ref, wuv_ref, o_ref, m_ref, acc_ref, *, tq):
    i = pl.program_id(1)
    rep = lambda x, w: jnp.concatenate([x] * (w // 128), axis=1)
    m_ref[...] = jnp.full(m_ref.shape, -jnp.inf, F32)
    acc_ref[...] = jnp.zeros(acc_ref.shape, F32)

    def update(kb, diagonal):
        k = k_ref[pl.ds(pl.multiple_of(kb * tq, tq), tq), :]
        for h in range(MLA_H):
            s = _dot_nt(q_ref[h], k) * ATTN_SCALE
            if diagonal:
                t = lax.broadcasted_iota(jnp.int32, (tq, tq), 0)
                c = lax.broadcasted_iota(jnp.int32, (tq, tq), 1)
                s = jnp.where(c <= t, s, -jnp.inf)
            m_prev = m_ref[h]
            m_new = jnp.maximum(m_prev, jnp.max(s, -1, keepdims=True))
            p = jnp.exp(s - rep(m_new, tq))
            acc_ref[h] = rep(jnp.exp(m_prev - m_new), KW) * acc_ref[h] + _dot(p.astype(BF16), k)
            m_ref[h] = m_new

    def body(kb, carry):
        update(kb, False)
        return carry

    lax.fori_loop(0, i, body, 0)
    update(i, True)
    y = None
    for h in range(MLA_H):
        acc = acc_ref[h]
        o = (acc[:, :MLA_KV] / acc[:, KW - 1:KW]).astype(BF16)
        yh = _dot(o, wuv_ref[h])
        y = yh if y is None else y + yh
    o_ref[...] = y


def _attn_prompt(q16, kcat, wuv, l, bp, t):
    tq = _pick(t, (512, 256))
    nq = t // tq
    return pl.pallas_call(
        functools.partial(_attn_prompt_kernel, tq=tq),
        out_shape=jax.ShapeDtypeStruct((bp * t, BW), F32),
        grid=(bp, nq),
        in_specs=[pl.BlockSpec((MLA_H, tq, KW), lambda b, i: (0, b * nq + i, 0)),
                  pl.BlockSpec((t, KW), lambda b, i: (b, 0)),
                  _layer_spec((MLA_H, MLA_KV, BW), l)],
        out_specs=pl.BlockSpec((tq, BW), lambda b, i: (b * nq + i, 0)),
        scratch_shapes=[pltpu.VMEM((MLA_H, tq, 128), F32), pltpu.VMEM((MLA_H, tq, KW), F32)],
        compiler_params=_params(("parallel", "parallel")),
        name="mla_prompt",
    )(q16, kcat, wuv)


def _attn_sample_kernel(pt_ref, q_ref, cn_ref, pn_ref, ckv_hbm, kpe_hbm, wuv_ref, o_ref,
                        kbuf, pbuf, sem, *, l, ts, n_pages):
    b = pl.program_id(0)
    slot = b % 2

    last = pl.num_programs(0) - 1
    nxt = jnp.minimum(b + 1, last)

    def page_copies(bb, p, sl):
        pg = pt_ref[bb, p]
        dst = pl.ds(p * PAGE, PAGE) if isinstance(p, int) else pl.ds(pl.multiple_of(p * PAGE, PAGE), PAGE)
        return (pltpu.make_async_copy(ckv_hbm.at[l, pg], kbuf.at[sl, dst, :], sem.at[0, sl]),
                pltpu.make_async_copy(kpe_hbm.at[l, pg], pbuf.at[sl, :, dst], sem.at[1, sl]))

    def fetch_loop(bb, sl, wait):
        def body(p, carry):
            for cp in page_copies(bb, p, sl):
                cp.wait() if wait else cp.start()
            return carry
        lax.fori_loop(0, n_pages, body, 0)

    @pl.when(b == 0)
    def _():
        fetch_loop(0, 0, False)

    fetch_loop(b, slot, True)
    for p in range(n_pages):
        for cp in page_copies(nxt, p, 1 - slot):
            cp.start()

    rows = MLA_H * ts
    q = q_ref[...].reshape(rows, QK)
    ql = q[:, :MLA_KV].astype(BF16)
    qp = q[:, MLA_KV:].astype(BF16)
    kc = kbuf[slot].astype(BF16)
    kp = pbuf[slot].astype(BF16)
    cn = cn_ref[...].astype(BF16)
    pn = pn_ref[...].astype(BF16)
    s = (_dot_nt(ql, kc) + _dot(qp, kp)) * ATTN_SCALE
    sn = (_dot_nt(ql, cn) + _dot_nt(qp, pn)) * ATTN_SCALE
    t = lax.broadcasted_iota(jnp.int32, (rows, ts), 0) % ts
    c = lax.broadcasted_iota(jnp.int32, (rows, ts), 1)
    sn = jnp.where(c <= t, sn, -jnp.inf)
    m = jnp.maximum(jnp.max(s, -1, keepdims=True), jnp.max(sn, -1, keepdims=True))
    p = jnp.exp(s - m)
    pnew = jnp.exp(sn - m)
    den = jnp.sum(p, -1, keepdims=True) + jnp.sum(pnew, -1, keepdims=True)
    o = ((_dot(p.astype(BF16), kc) + _dot(pnew.astype(BF16), cn)) / den).astype(BF16)
    y = _dot(o[0:ts], wuv_ref[0])
    for h in range(1, MLA_H):
        y += _dot(o[h * ts:(h + 1) * ts], wuv_ref[h])
    o_ref[...] = y

    @pl.when(b == last)
    def _():
        fetch_loop(nxt, 1 - slot, True)


def _attn_sample(page_table, qcat, ckv, kpe, cache_ckv, cache_kpe, wuv, l, row0, bs, ts):
    n_pages = page_table.shape[1]
    past = n_pages * PAGE
    blk0 = row0 // ts
    grid_spec = pltpu.PrefetchScalarGridSpec(
        num_scalar_prefetch=1,
        grid=(bs,),
        in_specs=[pl.BlockSpec((MLA_H, ts, QK), lambda b, pt: (0, blk0 + b, 0)),
                  pl.BlockSpec((ts, MLA_KV), lambda b, pt: (blk0 + b, 0)),
                  pl.BlockSpec((ts, MLA_ROPE), lambda b, pt: (blk0 + b, 0)),
                  pl.BlockSpec(memory_space=pl.ANY), pl.BlockSpec(memory_space=pl.ANY),
                  pl.BlockSpec((None, MLA_H, MLA_KV, BW), lambda b, pt: (l, 0, 0, 0))],
        out_specs=pl.BlockSpec((ts, BW), lambda b, pt: (b, 0)),
        scratch_shapes=[pltpu.VMEM((2, past, MLA_KV), F32), pltpu.VMEM((2, MLA_ROPE, past), F32),
                        pltpu.SemaphoreType.DMA((2, 2))])
    return pl.pallas_call(
        functools.partial(_attn_sample_kernel, l=l, ts=ts, n_pages=n_pages),
        out_shape=jax.ShapeDtypeStruct((bs * ts, BW), F32),
        grid_spec=grid_spec,
        compiler_params=_params(("arbitrary",)),
        name="mla_sample",
    )(page_table, qcat, ckv, kpe, cache_ckv, cache_kpe, wuv)


def _split3(x):
    hi = x.astype(BF16)
    r = x - hi.astype(F32)
    mid = r.astype(BF16)
    lo = (r - mid.astype(F32)).astype(BF16)
    return hi, mid, lo


def _dot_01x(m, x):
    hi, mid, lo = _split3(x)
    return _dot(m, hi) + _dot(m, mid) + _dot(m, lo)


def _dot_x01(x, m):
    hi, mid, lo = _split3(x)
    return _dot(hi, m) + _dot(mid, m) + _dot(lo, m)


def _bdot(a, b):
    return _dot(a.astype(BF16), b.astype(BF16))


def _rwkv_kernel(*refs, nz, per, c):
    zr_refs = refs[:nz]
    (sh0_ref, s0_ref, mu_ref, w0_ref, w2_ref, a0_ref, a2_ref, g2_ref, kk_ref, ka_ref,
     rk_ref, lng_ref, lnb_ref, y_ref, s_ref, zb_ref) = refs[nz:]
    bb = nz * per
    j = pl.program_id(1)

    @pl.when(j == 0)
    def _():
        s_ref[...] = s0_ref[...]
        zb_ref[:, 7:8, :] = sh0_ref[...]

    cw = RW_H * c
    bd_rows = (lax.broadcasted_iota(jnp.int32, (cw, BW), 0) // c
               == lax.broadcasted_iota(jnp.int32, (cw, BW), 1) // RW_HD)
    bd_sq = (lax.broadcasted_iota(jnp.int32, (cw, cw), 0) // c
             == lax.broadcasted_iota(jnp.int32, (cw, cw), 1) // c)
    bd_state = (lax.broadcasted_iota(jnp.int32, (BW, BW), 0) // RW_HD
                == lax.broadcasted_iota(jnp.int32, (BW, BW), 1) // RW_HD)
    ones_bd = jnp.where(bd_state, 1.0, 0.0).astype(BF16)
    tt = lax.broadcasted_iota(jnp.int32, (c, cw), 0)
    ss = lax.broadcasted_iota(jnp.int32, (c, cw), 1) % c
    strict = ss < tt
    incl = ss <= tt
    tri = jnp.where(lax.broadcasted_iota(jnp.int32, (c, c), 1)
                    <= lax.broadcasted_iota(jnp.int32, (c, c), 0), 1.0, 0.0).astype(BF16)

    def tile_bd(x, mask):
        return jnp.where(mask, jnp.concatenate([x] * RW_H, axis=0), 0.0).astype(BF16)

    for b in range(bb):
        z = zr_refs[b // per][(b % per) * c:(b % per + 1) * c, :]
        zb_ref[b, 8:8 + c, :] = z
        prev = zb_ref[b, 7:7 + c, :]
        zb_ref[b, 7:8, :] = z[c - 1:c, :]
        zm = z + (prev - z) * mu_ref[...]
        r = zm[:, 0:BW]
        k = zm[:, BW:2 * BW]
        v = zm[:, 2 * BW:3 * BW]
        lo = zm[:, 3 * BW:]
        xw = -(w0_ref[...] + _dot(jnp.tanh(lo).astype(BF16), w2_ref[...]))
        softplus = jnp.maximum(xw, 0.0) + jnp.log(1.0 + jnp.exp(-jnp.abs(xw)))
        logw = -jnp.exp(-softplus - 0.5)
        a_sig = _sigmoid(a0_ref[...] + _dot(lo.astype(BF16), a2_ref[...]))
        g = _dot(_sigmoid(lo).astype(BF16), g2_ref[...])
        kk = k * kk_ref[...]
        kk = kk / jnp.maximum(jnp.sqrt(_dot_x01(kk * kk, ones_bd)), 1e-12)
        k = k * (1.0 + (a_sig - 1.0) * ka_ref[...])
        a = -kk
        bm = kk * a_sig

        cl = _dot_01x(tri, logw)
        cl_last = cl[c - 1:c, :]
        e_out = jnp.exp(-cl)
        e_end = jnp.exp(cl_last - cl)
        ar = jnp.concatenate([a * jnp.exp(cl - logw), r * jnp.exp(cl)], axis=0).astype(BF16)
        kb_bd = jnp.concatenate([tile_bd(k * e_out, bd_rows), tile_bd(bm * e_out, bd_rows)], axis=0)
        gm = _dot_nt(ar, kb_bd)
        l_ak = jnp.where(strict, gm[0:c, 0:cw], 0.0)
        l_ab = jnp.where(strict, gm[0:c, cw:2 * cw], 0.0)
        m_rk = jnp.where(incl, gm[c:2 * c, 0:cw], 0.0)
        m_rb = jnp.where(incl, gm[c:2 * c, cw:2 * cw], 0.0)
        lv = _bdot(jnp.concatenate([l_ak, m_rk], axis=0), tile_bd(v, bd_rows))
        p, q = l_ab, l_ab
        for _ in range(max(1, (c - 1).bit_length())):
            pq = _bdot(jnp.concatenate([p, q], axis=0), tile_bd(q, bd_sq))
            p = p + pq[0:c]
            q = pq[c:2 * c]
        kb_end = jnp.concatenate([k * e_end, bm * e_end], axis=0).astype(BF16)
        decay = jnp.exp(cl_last)

        s_big = s_ref[b]
        xs = _dot_nt(ar, s_big.astype(BF16))
        x0 = xs[0:c] + lv[0:c]
        sa = x0 + _bdot(p, tile_bd(x0, bd_rows))
        s_new = s_big * decay + _dot_tn(jnp.concatenate([v, sa], axis=0).astype(BF16), kb_end)
        s_ref[b] = jnp.where(bd_state, s_new, 0.0)
        y = xs[c:2 * c] + lv[c:2 * c] + _bdot(m_rb, tile_bd(sa, bd_rows))

        mean = _dot_x01(y, ones_bd) * (1.0 / RW_HD)
        yc = y - mean
        var = _dot_x01(yc * yc, ones_bd) * (1.0 / RW_HD)
        yn = yc * lax.rsqrt(var + GN_EPS) * lng_ref[...] + lnb_ref[...]
        bonus = _dot_x01(r * k * rk_ref[...], ones_bd) * v
        y_ref[b] = (yn + bonus) * g


def _rwkv(zr, row0, bsz, t, shift0, s0, rp, l, bb, c):
    vec = lambda w: _layer_spec((1, w), l)
    if t == c:
        nz, per = 1, bb
        blk0 = row0 // (bb * c)
        z_specs = [pl.BlockSpec((bb * c, RW_IN), lambda i, j: (blk0 + i, 0))]
    else:
        nz, per = bb, 1
        z_specs = [pl.BlockSpec((c, RW_IN), lambda i, j, b=b: ((row0 + (i * bb + b) * t) // c + j, 0))
                   for b in range(bb)]
    return pl.pallas_call(
        functools.partial(_rwkv_kernel, nz=nz, per=per, c=c),
        out_shape=(jax.ShapeDtypeStruct((bsz, t, BW), F32), jax.ShapeDtypeStruct((bsz, BW, BW), F32)),
        grid=(bsz // bb, t // c),
        in_specs=z_specs + [
                  pl.BlockSpec((bb, 1, RW_IN), lambda i, j: (i, 0, 0)),
                  pl.BlockSpec((bb, BW, BW), lambda i, j: (i, 0, 0)),
                  vec(RW_IN), vec(BW), _layer_spec((128, BW), l), vec(BW), _layer_spec((128, BW), l),
                  _layer_spec((128, BW), l), vec(BW), vec(BW), vec(BW), vec(BW), vec(BW)],
        out_specs=(pl.BlockSpec((bb, c, BW), lambda i, j: (i, j, 0)),
                   pl.BlockSpec((bb, BW, BW), lambda i, j: (i, 0, 0))),
        scratch_shapes=[pltpu.VMEM((bb, 8 + c, RW_IN), F32)],
        compiler_params=_params(("parallel", "arbitrary")),
        name="rwkv7",
    )(*([zr] * nz), shift0, s0, rp["mu"], rp["w0"], rp["w2"], rp["a0"], rp["a2"], rp["g2"], rp["k_k"], rp["k_a"],
      rp["r_k"], rp["ln_g"], rp["ln_b"])


def _conv_kernel(u_ref, buf_ref, w_ref, cb_ref, g_ref, b_ref, y_ref, nb_ref, xp_ref, *, tt):
    j = pl.program_id(1)
    halo = CONV_K - 1
    top = 32 - halo

    @pl.when(j == 0)
    def _():
        xp_ref[:, top:32, :] = buf_ref[...]

    xp_ref[:, 32:32 + tt, :] = u_ref[...].reshape(xp_ref.shape[0], tt, BW)
    acc = xp_ref[:, top:top + tt, :] * w_ref[0:1, :] + cb_ref[...]
    for k in range(1, CONV_K):
        acc = acc + xp_ref[:, top + k:top + k + tt, :] * w_ref[k:k + 1, :]
    yn = _layer_norm(acc, g_ref[...], b_ref[...])
    y_ref[...] = yn * _sigmoid(yn)
    new = xp_ref[:, top + tt:32 + tt, :]
    xp_ref[:, top:32, :] = new
    nb_ref[...] = new


def _conv(u, row0, bsz, t, buf, cp, l, bb, tt):
    assert bb == 1 or tt == t
    halo = CONV_K - 1
    nt = t // tt
    blk0 = row0 // (bb * tt)
    vec = lambda: _layer_spec((1, BW), l)
    return pl.pallas_call(
        functools.partial(_conv_kernel, tt=tt),
        out_shape=(jax.ShapeDtypeStruct((bsz, t, BW), F32), jax.ShapeDtypeStruct((bsz, halo, BW), F32)),
        grid=(bsz // bb, t // tt),
        in_specs=[pl.BlockSpec((bb * tt, BW), lambda i, j: (blk0 + i * nt + j, 0)),
                  pl.BlockSpec((bb, halo, BW), lambda i, j: (i, 0, 0)),
                  _layer_spec((CONV_K, BW), l), vec(), vec(), vec()],
        out_specs=(pl.BlockSpec((bb, tt, BW), lambda i, j: (i, j, 0)),
                   pl.BlockSpec((bb, halo, BW), lambda i, j: (i, 0, 0))),
        scratch_shapes=[pltpu.VMEM((bb, 32 + tt, BW), F32)],
        compiler_params=_params(("parallel", "arbitrary")),
        name="conv_module",
    )(u, buf, cp["w"], cp["b"], cp["ln_g"], cp["ln_b"])


def _gelu_tanh(x):
    return 0.5 * x * (1.0 + jnp.tanh(math.sqrt(2.0 / math.pi) * (x + 0.044715 * (x * x * x))))


def _s5_out(u, hre, him, cre_ref, cim_ref, d_ref, gw_ref, gb_ref):
    y = _dot(hre.astype(BF16), cre_ref[...]) - _dot(him.astype(BF16), cim_ref[...]) + d_ref[...] * u
    y = _gelu_tanh(y)
    return y * _sigmoid(_dot(y.astype(BF16), gw_ref[...]) + gb_ref[...])


def _s5_prompt_kernel(u_ref, bre_ref, bim_ref, pre_ref, pim_ref, cre_ref, cim_ref, d_ref, gw_ref, gb_ref,
                      y_ref, hre_ref, him_ref, *, tt):
    j = pl.program_id(1)

    @pl.when(j == 0)
    def _():
        hre_ref[...] = jnp.zeros_like(hre_ref)
        him_ref[...] = jnp.zeros_like(him_ref)

    u = u_ref[...]
    ub = u.astype(BF16)
    xre = _dot(ub, bre_ref[...])
    xim = _dot(ub, bim_ref[...])
    row = lax.broadcasted_iota(jnp.int32, (tt, S5_N), 0)
    lre = pre_ref[0:1, :]
    lim = pim_ref[0:1, :]
    cre = lre * hre_ref[...] - lim * him_ref[...]
    cim = lre * him_ref[...] + lim * hre_ref[...]
    xre = jnp.where(row == 0, xre + cre, xre)
    xim = jnp.where(row == 0, xim + cim, xim)
    for s in range(tt.bit_length() - 1):
        d = 1 << s
        are = pre_ref[s:s + 1, :]
        aim = pim_ref[s:s + 1, :]
        sre = pltpu.roll(xre, d, 0)
        sim = pltpu.roll(xim, d, 0)
        keep = row >= d
        xre, xim = (xre + jnp.where(keep, are * sre - aim * sim, 0.0),
                    xim + jnp.where(keep, are * sim + aim * sre, 0.0))
    hre_ref[...] = xre[tt - 1:tt, :]
    him_ref[...] = xim[tt - 1:tt, :]
    y_ref[...] = _s5_out(u, xre, xim, cre_ref, cim_ref, d_ref, gw_ref, gb_ref)


def _s5_prompt(u, bsz, t, sp, l, tt):
    npow = sp["pre"].shape[1]
    nt = t // tt
    return pl.pallas_call(
        functools.partial(_s5_prompt_kernel, tt=tt),
        out_shape=(jax.ShapeDtypeStruct((bsz, t, BW), F32),
                   jax.ShapeDtypeStruct((bsz, 1, S5_N), F32), jax.ShapeDtypeStruct((bsz, 1, S5_N), F32)),
        grid=(bsz, nt),
        in_specs=[pl.BlockSpec((tt, BW), lambda i, j: (i * nt + j, 0)),
                  _layer_spec((BW, S5_N), l), _layer_spec((BW, S5_N), l),
                  _layer_spec((npow, S5_N), l), _layer_spec((npow, S5_N), l),
                  _layer_spec((S5_N, BW), l), _layer_spec((S5_N, BW), l),
                  _layer_spec((1, BW), l), _layer_spec((BW, BW), l), _layer_spec((1, BW), l)],
        out_specs=(pl.BlockSpec((None, tt, BW), lambda i, j: (i, j, 0)),
                   pl.BlockSpec((None, 1, S5_N), lambda i, j: (i, 0, 0)),
                   pl.BlockSpec((None, 1, S5_N), lambda i, j: (i, 0, 0))),
        compiler_params=_params(("parallel", "arbitrary")),
        name="s5_prompt",
    )(u, sp["bre"], sp["bim"], sp["pre"], sp["pim"], sp["cre"], sp["cim"], sp["d"], sp["glu_w"], sp["glu_b"])


def _s5_sample_kernel(u_ref, h0re_ref, h0im_ref, bre_ref, bim_ref, pre_ref, pim_ref, cre_ref, cim_ref,
                      d_ref, gw_ref, gb_ref, y_ref, hre_ref, him_ref, *, ts):
    lre = pre_ref[0:1, :]
    lim = pim_ref[0:1, :]
    hre = h0re_ref[...]
    him = h0im_ref[...]
    for t in range(ts):
        u = u_ref[t]
        ub = u.astype(BF16)
        hre, him = (lre * hre - lim * him + _dot(ub, bre_ref[...]),
                    lre * him + lim * hre + _dot(ub, bim_ref[...]))
        y_ref[t] = _s5_out(u, hre, him, cre_ref, cim_ref, d_ref, gw_ref, gb_ref)
    hre_ref[...] = hre
    him_ref[...] = him


def _s5_sample(u_tm, h0re, h0im, sp, l):
    ts, bsz, _ = u_tm.shape
    npow = sp["pre"].shape[1]
    return pl.pallas_call(
        functools.partial(_s5_sample_kernel, ts=ts),
        out_shape=(jax.ShapeDtypeStruct((ts, bsz, BW), F32),
                   jax.ShapeDtypeStruct((bsz, S5_N), F32), jax.ShapeDtypeStruct((bsz, S5_N), F32)),
        grid=(1,),
        in_specs=[_const_spec((ts, bsz, BW)), _const_spec((bsz, S5_N)), _const_spec((bsz, S5_N)),
                  _layer_spec((BW, S5_N), l), _layer_spec((BW, S5_N), l),
                  _layer_spec((npow, S5_N), l), _layer_spec((npow, S5_N), l),
                  _layer_spec((S5_N, BW), l), _layer_spec((S5_N, BW), l),
                  _layer_spec((1, BW), l), _layer_spec((BW, BW), l), _layer_spec((1, BW), l)],
        out_specs=(_const_spec((ts, bsz, BW)), _const_spec((bsz, S5_N)), _const_spec((bsz, S5_N))),
        compiler_params=_params(("arbitrary",)),
        name="s5_sample",
    )(u_tm, h0re, h0im, sp["bre"], sp["bim"], sp["pre"], sp["pim"], sp["cre"], sp["cim"], sp["d"],
      sp["glu_w"], sp["glu_b"])


def _merge_kernel(h_ref, *refs, p_tiles):
    yp_refs, ys_refs = refs[0:N_BRANCH], refs[N_BRANCH:2 * N_BRANCH]
    wg_ref, bg_ref, wbr_ref, wo_ref, g_ref, b_ref, o_ref = refs[2 * N_BRANCH:]
    is_prompt = pl.program_id(0) < p_tiles
    h = h_ref[...]
    hb = h.astype(BF16)
    m = None
    for n in range(N_BRANCH):
        y = jnp.where(is_prompt, yp_refs[n][...], ys_refs[n][...])
        gate = _sigmoid(_dot(hb, wg_ref[:, n * D_MODEL:(n + 1) * D_MODEL]) + bg_ref[n:n + 1, :])
        term = gate * _dot(y.astype(BF16), wbr_ref[n])
        m = term if m is None else m + term
    out = _dot(m.astype(BF16), wo_ref[...])
    o_ref[...] = _layer_norm(ALPHA * h + out, g_ref[...], b_ref[...])


def _merge(h, ys_p, ys_s, mw, l):
    n = h.shape[0]
    n_p, n_s = ys_p[0].shape[0], ys_s[0].shape[0]
    tm = _pick(math.gcd(n_p, n_s), (512, 256, 128, 64, 32, 16, 8))
    p_tiles, s_tiles = n_p // tm, n_s // tm
    row = lambda w: pl.BlockSpec((tm, w), lambda i: (i, 0))
    p_row = pl.BlockSpec((tm, BW), lambda i: (jnp.minimum(i, p_tiles - 1), 0))
    s_row = pl.BlockSpec((tm, BW), lambda i: (jnp.clip(i - p_tiles, 0, s_tiles - 1), 0))
    return pl.pallas_call(
        functools.partial(_merge_kernel, p_tiles=p_tiles),
        out_shape=jax.ShapeDtypeStruct((n, D_MODEL), F32),
        grid=(n // tm,),
        in_specs=[row(D_MODEL)] + [p_row] * N_BRANCH + [s_row] * N_BRANCH + [
                  _layer_spec((D_MODEL, N_BRANCH * D_MODEL), l), _layer_spec((N_BRANCH, D_MODEL), l),
                  _layer_spec((N_BRANCH, BW, D_MODEL), l), _layer_spec((D_MODEL, D_MODEL), l),
                  _layer_spec((1, D_MODEL), l), _layer_spec((1, D_MODEL), l)],
        out_specs=row(D_MODEL),
        compiler_params=_params(("parallel",)),
        name="branch_merge",
    )(h, *ys_p, *ys_s, mw["wg"], mw["bg"], mw["wbr"], mw["wo"], mw["g"], mw["b"])


def _pad_rows(w, lo, total):
    return jnp.pad(w, ((0, 0), (lo, total - lo - w.shape[1]), (0, 0)))


def _s5_params(a_re, a_im, log_dt, b_re, b_im, c_re, c_im, npow):
    dep = a_re.shape[0]
    dt = jnp.exp(log_dt)[..., None]
    mag = jnp.exp(a_re * dt)
    lam_re, lam_im = mag * jnp.cos(a_im * dt), mag * jnp.sin(a_im * dt)
    den = a_re * a_re + a_im * a_im
    co_re = ((lam_re - 1.0) * a_re + lam_im * a_im) / den
    co_im = (lam_im * a_re - (lam_re - 1.0) * a_im) / den
    bb_re = co_re[..., None] * b_re - co_im[..., None] * b_im
    bb_im = co_re[..., None] * b_im + co_im[..., None] * b_re
    eye = jnp.eye(S5_GROUPS, dtype=F32)
    def bmat(x):
        return jnp.einsum("lgpc,gk->lgckp", x, eye).reshape(dep, BW, S5_N)
    def cmat(x):
        return jnp.einsum("lgcp,gk->lgpkc", x, eye).reshape(dep, S5_N, BW)
    pre, pim = [lam_re.reshape(dep, S5_N)], [lam_im.reshape(dep, S5_N)]
    for _ in range(npow - 1):
        xr, xi = pre[-1], pim[-1]
        pre.append(xr * xr - xi * xi)
        pim.append(2.0 * xr * xi)
    return dict(bre=bmat(bb_re).astype(BF16), bim=bmat(bb_im).astype(BF16),
                pre=jnp.stack(pre, axis=1), pim=jnp.stack(pim, axis=1),
                cre=cmat(c_re).astype(BF16), cim=cmat(c_im).astype(BF16))


def _rope_tables(pos):
    inv = 1.0 / (ROPE_BASE ** (jnp.arange(0, MLA_ROPE, 2, dtype=F32) / MLA_ROPE))
    ang = pos.astype(F32)[:, None] * inv[None, :]
    cos, sin = jnp.cos(ang), jnp.sin(ang)
    return jnp.concatenate([cos, cos], -1), jnp.concatenate([-sin, sin], -1)


def kernel(x_prompt, x_sample, cache_ckv, cache_kpe, state_rwkv, state_rwkv_shift, state_conv, state_s5_re, state_s5_im, page_table, ffn1_w_in, ffn1_w_down, ln1_g, ln1_b, w_in, b_gate, mla_q_norm, mla_w_uq, mla_kv_norm, mla_w_uk, mla_w_uv, rwkv_mu, rwkv_w0, rwkv_w2, rwkv_a0, rwkv_a2, rwkv_g2, rwkv_k_k, rwkv_k_a, rwkv_r_k, rwkv_ln_g, rwkv_ln_b, conv_w, conv_b, conv_ln_g, conv_ln_b, s5_a_re, s5_a_im, s5_log_dt, s5_b_re, s5_b_im, s5_c_re, s5_c_im, s5_d, s5_glu_w, s5_glu_b, w_branch, w_out, ln2_g, ln2_b, ffn2_w_in, ffn2_w_down, ln3_g, ln3_b):
    bp, tp, _ = x_prompt.shape
    bs, ts, _ = x_sample.shape
    depth = w_in.shape[0]
    n_p, n_s = bp * tp, bs * ts
    past = page_table.shape[1] * PAGE
    halo = CONV_K - 1
    vec = lambda w: w[:, None, :]

    o_kv = MLA_Q
    o_kp = o_kv + MLA_KV
    o_r = o_kp + MLA_ROPE
    o_c = o_r + RW_IN
    o_s = o_c + 2 * BW
    o_g = o_s + BW
    half = MLA_ROPE // 2
    wkp = w_in[:, :, o_kp:o_r]
    uq = mla_w_uq.reshape(depth, MLA_Q, MLA_H, MLA_NOPE + MLA_ROPE)
    uq_pe = uq[..., MLA_NOPE:]
    swap = lambda x: jnp.concatenate([x[..., half:], x[..., :half]], -1)
    to_heads = lambda x: jnp.moveaxis(x, 2, 1).astype(BF16)
    pw = dict(wq=w_in[:, :, :o_kv].astype(BF16), wkv=w_in[:, :, o_kv:o_kp].astype(BF16),
              wkp=wkp.astype(BF16), wkps=swap(wkp).astype(BF16),
              wr=w_in[:, :, o_r:o_c].astype(BF16), wc=w_in[:, :, o_c:o_s].astype(BF16),
              ws=w_in[:, :, o_s:o_g].astype(BF16),
              gq=vec(mla_q_norm), gkv=vec(mla_kv_norm),
              wn=to_heads(uq[..., :MLA_NOPE]), wp=to_heads(uq_pe), wps=to_heads(swap(uq_pe)),
              wuk=jnp.transpose(mla_w_uk, (0, 2, 3, 1)).astype(BF16))
    eye_h = jnp.eye(MLA_H, dtype=F32)
    wuv = jnp.einsum("lrhv,hk->lhrkv", mla_w_uv, eye_h).reshape(depth, MLA_H, MLA_KV, BW).astype(BF16)
    rp = dict(mu=vec(rwkv_mu), w0=vec(rwkv_w0), a0=vec(rwkv_a0),
              w2=_pad_rows(rwkv_w2, 0, 128).astype(BF16), a2=_pad_rows(rwkv_a2, 32, 128).astype(BF16),
              g2=_pad_rows(rwkv_g2, 64, 128).astype(BF16),
              k_k=vec(rwkv_k_k), k_a=vec(rwkv_k_a), r_k=vec(rwkv_r_k.reshape(depth, BW)),
              ln_g=vec(rwkv_ln_g), ln_b=vec(rwkv_ln_b))
    cp = dict(w=conv_w, b=vec(conv_b), ln_g=vec(conv_ln_g), ln_b=vec(conv_ln_b))
    tt_s5 = _pick(tp, (512, 256, 128))
    sp = _s5_params(s5_a_re, s5_a_im, s5_log_dt, s5_b_re, s5_b_im, s5_c_re, s5_c_im, max(1, tt_s5.bit_length() - 1))
    sp.update(d=vec(s5_d), glu_w=s5_glu_w.astype(BF16), glu_b=vec(s5_glu_b))
    mw = dict(wg=w_in[:, :, o_g:].astype(BF16), bg=b_gate, wbr=w_branch.astype(BF16), wo=w_out.astype(BF16),
              g=vec(ln2_g), b=vec(ln2_b))
    f1 = (ffn1_w_in.astype(BF16), ffn1_w_down.astype(BF16), vec(ln1_g), vec(ln1_b))
    f2 = (ffn2_w_in.astype(BF16), ffn2_w_down.astype(BF16), vec(ln3_g), vec(ln3_b))

    cos_p, sin_p = _rope_tables(jnp.arange(tp))
    cos_s, sin_s = _rope_tables(past + jnp.arange(ts))
    cos2 = jnp.concatenate([jnp.tile(cos_p, (bp, 1)), jnp.tile(cos_s, (bs, 1))], 0)
    sin2 = jnp.concatenate([jnp.tile(sin_p, (bp, 1)), jnp.tile(sin_s, (bs, 1))], 0)

    s_in = jnp.einsum("lbhij,hk->lbhikj", state_rwkv, jnp.eye(RW_H, dtype=F32)).reshape(depth, bs, BW, BW)
    c_rw = _pick(tp, (64, 32, 16, 8))
    bb_rw = _pick(bs, (4, 2, 1))
    tt_cv = _pick(tp, (512, 256, 128, 64, 32))
    bb_cv = _pick(bs, (32, 16, 8, 4, 2, 1))

    cache_kpe_t = jnp.swapaxes(cache_kpe, 2, 3)
    x = jnp.concatenate([x_prompt.reshape(n_p, D_MODEL), x_sample.reshape(n_s, D_MODEL)], 0)
    outs = [[] for _ in range(14)]
    for l in range(depth):
        h = _ffn_ln(x, *f1, l)
        qcat, q16, ckv, kpe, kcat, zr, u, zs = _prep(h, cos2, sin2, pw, l)
        ya_p = _attn_prompt(q16, kcat, wuv, l, bp, tp)
        ya_s = _attn_sample(page_table, qcat, ckv, kpe, cache_ckv, cache_kpe_t, wuv, l, n_p, bs, ts)
        yb_p, srw_p = _rwkv(zr, 0, bp, tp, jnp.zeros((bp, 1, RW_IN), F32), jnp.zeros((bp, BW, BW), F32),
                            rp, l, bp, c_rw)
        yb_s, srw_s = _rwkv(zr, n_p, bs, ts, state_rwkv_shift[l][:, None, :], s_in[l], rp, l, bb_rw, ts)
        yc_p, cv_p = _conv(u, 0, bp, tp, jnp.zeros((bp, halo, BW), F32), cp, l, 1, tt_cv)
        yc_s, cv_s = _conv(u, n_p, bs, ts, state_conv[l], cp, l, bb_cv, ts)
        yd_p, s5re_p, s5im_p = _s5_prompt(zs, bp, tp, sp, l, tt_s5)
        yd_s, s5re_s, s5im_s = _s5_sample(jnp.swapaxes(zs[n_p:].reshape(bs, ts, BW), 0, 1),
                                          state_s5_re[l].reshape(bs, S5_N), state_s5_im[l].reshape(bs, S5_N), sp, l)
        yd_s = jnp.swapaxes(yd_s, 0, 1)
        flat = lambda a: a.reshape(-1, BW)
        x = _merge(h, (ya_p, flat(yb_p), flat(yc_p), flat(yd_p)), (ya_s, flat(yb_s), flat(yc_s), flat(yd_s)),
                   mw, l)
        x = _ffn_ln(x, *f2, l)
        zr_p = zr[:n_p].reshape(bp, tp, RW_IN)
        zr_s = zr[n_p:].reshape(bs, ts, RW_IN)

        diag = lambda s: jnp.stack([s[:, i * RW_HD:(i + 1) * RW_HD, i * RW_HD:(i + 1) * RW_HD]
                                    for i in range(RW_H)], 1)
        new_p = (ckv[:n_p].reshape(bp, tp, MLA_KV), kpe[:n_p].reshape(bp, tp, MLA_ROPE), diag(srw_p),
                 zr_p[:, -1], cv_p, s5re_p.reshape(bp, S5_GROUPS, S5_P), s5im_p.reshape(bp, S5_GROUPS, S5_P))
        new_s = (ckv[n_p:].reshape(bs, ts, MLA_KV), kpe[n_p:].reshape(bs, ts, MLA_ROPE), diag(srw_s),
                 zr_s[:, -1], cv_s, s5re_s.reshape(bs, S5_GROUPS, S5_P), s5im_s.reshape(bs, S5_GROUPS, S5_P))
        for i, a in enumerate(new_p + new_s):
            outs[i].append(a)
    return (x[:n_p].reshape(bp, tp, D_MODEL), x[n_p:].reshape(bs, ts, D_MODEL)) + tuple(jnp.stack(o) for o in outs)
```

```python
import functools
import math

import jax
import jax.numpy as jnp
from jax import lax
from jax.experimental import pallas as pl
from jax.experimental.pallas import tpu as pltpu

F32 = jnp.float32
BF16 = jnp.bfloat16
HI = lax.Precision.HIGHEST

D_MODEL = 1024
PAGE = 128
BW = D_MODEL // 4
N_BRANCH = 4
MLA_V = 64
MLA_H = BW // MLA_V
MLA_NOPE = 64
MLA_ROPE = 32
MLA_Q = D_MODEL // 4
MLA_KV = D_MODEL // 8
QK = MLA_KV + MLA_ROPE
KW = 256
ROPE_BASE = 10000.0
RW_HD = 64
RW_H = BW // RW_HD
RW_IN = 3 * BW + 128
CONV_K = 31
S5_G = 16
S5_GROUPS = BW // S5_G
S5_P = 64
S5_N = S5_GROUPS * S5_P
D_FF = 2816
FF_CHUNK = 256
DEPTH = 4
ALPHA = (2 * DEPTH) ** 0.25
ATTN_SCALE = (MLA_NOPE + MLA_ROPE) ** -0.5
LN_EPS = 1e-5
RMS_EPS = 1e-6
GN_EPS = 64e-5
VMEM_LIMIT = 56 * 1024 * 1024


def _pick(n, cands):
    for c in cands:
        if n % c == 0:
            return c
    raise ValueError(f"no tile in {cands} divides {n}")


def _const_spec(shape):
    nd = len(shape)
    return pl.BlockSpec(shape, lambda *_: (0,) * nd)


def _layer_spec(shape, l):
    nd = len(shape)
    return pl.BlockSpec((None,) + tuple(shape), lambda *_: (l,) + (0,) * nd)


def _params(sem, vmem=VMEM_LIMIT):
    return pltpu.CompilerParams(dimension_semantics=sem, vmem_limit_bytes=vmem)


def _dot(a, b, **kw):
    return jnp.dot(a, b, preferred_element_type=F32, **kw)


def _dot_nt(a, b, **kw):
    return lax.dot_general(a, b, (((1,), (1,)), ((), ())), preferred_element_type=F32, **kw)


def _dot_tn(a, b, **kw):
    return lax.dot_general(a, b, (((0,), (0,)), ((), ())), preferred_element_type=F32, **kw)


def _layer_norm(x, g, b):
    mu = jnp.mean(x, -1, keepdims=True)
    xc = x - mu
    var = jnp.mean(xc * xc, -1, keepdims=True)
    return xc * lax.rsqrt(var + LN_EPS) * g + b


def _rms_norm(x, g):
    return x * lax.rsqrt(jnp.mean(x * x, -1, keepdims=True) + RMS_EPS) * g


def _sigmoid(x):
    return 1.0 / (1.0 + jnp.exp(-x))


def _ffn_ln_kernel(x_ref, win_ref, wdn_ref, g_ref, b_ref, o_ref, acc_ref):
    x = x_ref[...]
    xb = x.astype(BF16)
    for c in range(D_FF // FF_CHUNK):
        lo = c * FF_CHUNK
        a = _dot(xb, win_ref[:, lo:lo + FF_CHUNK])
        b = _dot(xb, win_ref[:, D_FF + lo:D_FF + lo + FF_CHUNK])
        h = (a * _sigmoid(a) * b).astype(BF16)
        d = _dot(h, wdn_ref[lo:lo + FF_CHUNK, :])
        if c == 0:
            acc_ref[...] = d
        else:
            acc_ref[...] += d
    o_ref[...] = _layer_norm(ALPHA * x + 0.5 * acc_ref[...], g_ref[...], b_ref[...])


def _ffn_ln(x, w_in, w_down, g, b, l):
    n = x.shape[0]
    tm = _pick(n, (512, 256, 128, 64, 32, 16, 8))
    return pl.pallas_call(
        _ffn_ln_kernel,
        out_shape=jax.ShapeDtypeStruct((n, D_MODEL), F32),
        grid=(n // tm,),
        in_specs=[pl.BlockSpec((tm, D_MODEL), lambda i: (i, 0)),
                  _layer_spec((D_MODEL, 2 * D_FF), l), _layer_spec((D_FF, D_MODEL), l),
                  _layer_spec((1, D_MODEL), l), _layer_spec((1, D_MODEL), l)],
        out_specs=pl.BlockSpec((tm, D_MODEL), lambda i: (i, 0)),
        scratch_shapes=[pltpu.VMEM((tm, D_MODEL), F32)],
        compiler_params=_params(("parallel",)),
        name="ffn_ln",
    )(x, w_in, w_down, g, b)


def _prep_kernel(h_ref, cos_ref, sin_ref, wq_ref, wkv_ref, wkp_ref, wkps_ref, wr_ref, wc_ref, ws_ref,
                 gq_ref, gkv_ref, wn_ref, wp_ref, wps_ref, wuk_ref,
                 qcat_ref, q16_ref, ckv_ref, kpe_ref, kcat_ref, zr_ref, u_ref, zs_ref):
    hb = h_ref[...].astype(BF16)
    cos2 = cos_ref[...]
    sin2 = sin_ref[...]
    tm = hb.shape[0]
    zq = _rms_norm(_dot(hb, wq_ref[...]), gq_ref[...]).astype(BF16)
    for h in range(MLA_H):
        qn = _dot(zq, wn_ref[h]).astype(BF16)
        ql = _dot(qn, wuk_ref[h])
        qpe = _dot(zq, wp_ref[h]) * cos2 + _dot(zq, wps_ref[h]) * sin2
        qcat_ref[h, :, 0:MLA_KV] = ql
        qcat_ref[h, :, MLA_KV:QK] = qpe
        q16_ref[h, :, 0:MLA_KV] = ql.astype(BF16)
        q16_ref[h, :, MLA_KV:QK] = qpe.astype(BF16)
        q16_ref[h, :, QK:KW] = jnp.zeros((tm, KW - QK), BF16)
    ckv = _rms_norm(_dot(hb, wkv_ref[...]), gkv_ref[...])
    kpe = _dot(hb, wkp_ref[...]) * cos2 + _dot(hb, wkps_ref[...]) * sin2
    ckv_ref[...] = ckv
    kpe_ref[...] = kpe
    kcat_ref[:, 0:MLA_KV] = ckv.astype(BF16)
    kcat_ref[:, MLA_KV:QK] = kpe.astype(BF16)
    kcat_ref[:, QK:KW] = jnp.ones((tm, KW - QK), BF16)
    zr_ref[...] = _dot(hb, wr_ref[...])
    zc = _dot(hb, wc_ref[...])
    u_ref[...] = zc[:, :BW] * _sigmoid(zc[:, BW:])
    zs_ref[...] = _dot(hb, ws_ref[...])


def _prep(h, cos2, sin2, pw, l):
    n = h.shape[0]
    tm = _pick(n, (512, 256, 128, 64, 32, 16))
    row = lambda w: pl.BlockSpec((tm, w), lambda i: (i, 0))
    return pl.pallas_call(
        _prep_kernel,
        out_shape=(jax.ShapeDtypeStruct((MLA_H, n, QK), F32),
                   jax.ShapeDtypeStruct((MLA_H, n, KW), BF16),
                   jax.ShapeDtypeStruct((n, MLA_KV), F32),
                   jax.ShapeDtypeStruct((n, MLA_ROPE), F32),
                   jax.ShapeDtypeStruct((n, KW), BF16),
                   jax.ShapeDtypeStruct((n, RW_IN), F32),
                   jax.ShapeDtypeStruct((n, BW), F32),
                   jax.ShapeDtypeStruct((n, BW), F32)),
        grid=(n // tm,),
        in_specs=[row(D_MODEL), row(MLA_ROPE), row(MLA_ROPE),
                  _layer_spec((D_MODEL, MLA_Q), l), _layer_spec((D_MODEL, MLA_KV), l),
                  _layer_spec((D_MODEL, MLA_ROPE), l), _layer_spec((D_MODEL, MLA_ROPE), l),
                  _layer_spec((D_MODEL, RW_IN), l), _layer_spec((D_MODEL, 2 * BW), l),
                  _layer_spec((D_MODEL, BW), l),
                  _layer_spec((1, MLA_Q), l), _layer_spec((1, MLA_KV), l),
                  _layer_spec((MLA_H, MLA_Q, MLA_NOPE), l), _layer_spec((MLA_H, MLA_Q, MLA_ROPE), l),
                  _layer_spec((MLA_H, MLA_Q, MLA_ROPE), l), _layer_spec((MLA_H, MLA_NOPE, MLA_KV), l)],
        out_specs=(pl.BlockSpec((MLA_H, tm, QK), lambda i: (0, i, 0)),
                   pl.BlockSpec((MLA_H, tm, KW), lambda i: (0, i, 0)),
                   row(MLA_KV), row(MLA_ROPE), row(KW), row(RW_IN), row(BW), row(BW)),
        compiler_params=_params(("parallel",)),
        name="mixer_prep",
    )(h, cos2, sin2, pw["wq"], pw["wkv"], pw["wkp"], pw["wkps"], pw["wr"], pw["wc"], pw["ws"],
      pw["gq"], pw["gkv"], pw["wn"], pw["wp"], pw["wps"], pw["wuk"])


def _attn_prompt_kernel(q_ref, k_ref, wuv_ref, o_ref, m_ref, acc_ref, *, tq):
    i = pl.program_id(1)
    rep = lambda x, w: jnp.concatenate([x] * (w // 128), axis=1)
    m_ref[...] = jnp.full(m_ref.shape, -jnp.inf, F32)
    acc_ref[...] = jnp.zeros(acc_ref.shape, F32)

    def update(kb, diagonal):
        k = k_ref[pl.ds(pl.multiple_of(kb * tq, tq), tq), :]
        for h in range(MLA_H):
            s = _dot_nt(q_ref[h], k) * ATTN_SCALE
            if diagonal:
                t = lax.broadcasted_iota(jnp.int32, (tq, tq), 0)
                c = lax.broadcasted_iota(jnp.int32, (tq, tq), 1)
                s = jnp.where(c <= t, s, -jnp.inf)
            m_prev = m_ref[h]
            m_new = jnp.maximum(m_prev, jnp.max(s, -1, keepdims=True))
            p = jnp.exp(s - rep(m_new, tq))
            acc_ref[h] = rep(jnp.exp(m_prev - m_new), KW) * acc_ref[h] + _dot(p.astype(BF16), k)
            m_ref[h] = m_new

    def body(kb, carry):
        update(kb, False)
        return carry

    lax.fori_loop(0, i, body, 0)
    update(i, True)
    y = None
    for h in range(MLA_H):
        acc = acc_ref[h]
        o = (acc[:, :MLA_KV] / acc[:, KW - 1:KW]).astype(BF16)
        yh = _dot(o, wuv_ref[h])
        y = yh if y is None else y + yh
    o_ref[...] = y


def _attn_prompt(q16, kcat, wuv, l, bp, t):
    tq = _pick(t, (512, 256))
    nq = t // tq
    return pl.pallas_call(
        functools.partial(_attn_prompt_kernel, tq=tq),
        out_shape=jax.ShapeDtypeStruct((bp * t, BW), F32),
        grid=(bp, nq),
        in_specs=[pl.BlockSpec((MLA_H, tq, KW), lambda b, i: (0, b * nq + i, 0)),
                  pl.BlockSpec((t, KW), lambda b, i: (b, 0)),
                  _layer_spec((MLA_H, MLA_KV, BW), l)],
        out_specs=pl.BlockSpec((tq, BW), lambda b, i: (b * nq + i, 0)),
        scratch_shapes=[pltpu.VMEM((MLA_H, tq, 128), F32), pltpu.VMEM((MLA_H, tq, KW), F32)],
        compiler_params=_params(("parallel", "parallel")),
        name="mla_prompt",
    )(q16, kcat, wuv)


def _attn_sample_kernel(pt_ref, q_ref, cn_ref, pn_ref, ckv_hbm, kpe_hbm, wuv_ref, o_ref,
                        kbuf, pbuf, sem, *, l, ts, n_pages):
    b = pl.program_id(0)
    slot = b % 2

    last = pl.num_programs(0) - 1
    nxt = jnp.minimum(b + 1, last)

    def page_copies(bb, p, sl):
        pg = pt_ref[bb, p]
        dst = pl.ds(p * PAGE, PAGE) if isinstance(p, int) else pl.ds(pl.multiple_of(p * PAGE, PAGE), PAGE)
        return (pltpu.make_async_copy(ckv_hbm.at[l, pg], kbuf.at[sl, dst, :], sem.at[0, sl]),
                pltpu.make_async_copy(kpe_hbm.at[l, pg], pbuf.at[sl, :, dst], sem.at[1, sl]))

    def fetch_loop(bb, sl, wait):
        def body(p, carry):
            for cp in page_copies(bb, p, sl):
                cp.wait() if wait else cp.start()
            return carry
        lax.fori_loop(0, n_pages, body, 0)

    @pl.when(b == 0)
    def _():
        fetch_loop(0, 0, False)

    fetch_loop(b, slot, True)
    for p in range(n_pages):
        for cp in page_copies(nxt, p, 1 - slot):
            cp.start()

    rows = MLA_H * ts
    q = q_ref[...].reshape(rows, QK)
    ql = q[:, :MLA_KV].astype(BF16)
    qp = q[:, MLA_KV:].astype(BF16)
    kc = kbuf[slot].astype(BF16)
    kp = pbuf[slot].astype(BF16)
    cn = cn_ref[...].astype(BF16)
    pn = pn_ref[...].astype(BF16)
    s = (_dot_nt(ql, kc) + _dot(qp, kp)) * ATTN_SCALE
    sn = (_dot_nt(ql, cn) + _dot_nt(qp, pn)) * ATTN_SCALE
    t = lax.broadcasted_iota(jnp.int32, (rows, ts), 0) % ts
    c = lax.broadcasted_iota(jnp.int32, (rows, ts), 1)
    sn = jnp.where(c <= t, sn, -jnp.inf)
    m = jnp.maximum(jnp.max(s, -1, keepdims=True), jnp.max(sn, -1, keepdims=True))
    p = jnp.exp(s - m)
    pnew = jnp.exp(sn - m)
    den = jnp.sum(p, -1, keepdims=True) + jnp.sum(pnew, -1, keepdims=True)
    o = ((_dot(p.astype(BF16), kc) + _dot(pnew.astype(BF16), cn)) / den).astype(BF16)
    y = _dot(o[0:ts], wuv_ref[0])
    for h in range(1, MLA_H):
        y += _dot(o[h * ts:(h + 1) * ts], wuv_ref[h])
    o_ref[...] = y

    @pl.when(b == last)
    def _():
        fetch_loop(nxt, 1 - slot, True)


def _attn_sample(page_table, qcat, ckv, kpe, cache_ckv, cache_kpe, wuv, l, row0, bs, ts):
    n_pages = page_table.shape[1]
    past = n_pages * PAGE
    blk0 = row0 // ts
    grid_spec = pltpu.PrefetchScalarGridSpec(
        num_scalar_prefetch=1,
        grid=(bs,),
        in_specs=[pl.BlockSpec((MLA_H, ts, QK), lambda b, pt: (0, blk0 + b, 0)),
                  pl.BlockSpec((ts, MLA_KV), lambda b, pt: (blk0 + b, 0)),
                  pl.BlockSpec((ts, MLA_ROPE), lambda b, pt: (blk0 + b, 0)),
                  pl.BlockSpec(memory_space=pl.ANY), pl.BlockSpec(memory_space=pl.ANY),
                  pl.BlockSpec((None, MLA_H, MLA_KV, BW), lambda b, pt: (l, 0, 0, 0))],
        out_specs=pl.BlockSpec((ts, BW), lambda b, pt: (b, 0)),
        scratch_shapes=[pltpu.VMEM((2, past, MLA_KV), F32), pltpu.VMEM((2, MLA_ROPE, past), F32),
                        pltpu.SemaphoreType.DMA((2, 2))])
    return pl.pallas_call(
        functools.partial(_attn_sample_kernel, l=l, ts=ts, n_pages=n_pages),
        out_shape=jax.ShapeDtypeStruct((bs * ts, BW), F32),
        grid_spec=grid_spec,
        compiler_params=_params(("arbitrary",)),
        name="mla_sample",
    )(page_table, qcat, ckv, kpe, cache_ckv, cache_kpe, wuv)


def _split3(x):
    hi = x.astype(BF16)
    r = x - hi.astype(F32)
    mid = r.astype(BF16)
    lo = (r - mid.astype(F32)).astype(BF16)
    return hi, mid, lo


def _dot_01x(m, x):
    w = x.shape[1]
    r = _dot(m, jnp.concatenate(_split3(x), axis=1))
    return r[:, 0:w] + r[:, w:2 * w] + r[:, 2 * w:3 * w]


def _dot_x01(x, m):
    n = x.shape[0]
    r = _dot(jnp.concatenate(_split3(x), axis=0), m)
    return r[0:n] + r[n:2 * n] + r[2 * n:3 * n]


def _bdot(a, b):
    return _dot(a.astype(BF16), b.astype(BF16))


def _iota(shape, dim):
    return lax.broadcasted_iota(jnp.int32, shape, dim)


def _block_diag_mask(rows, row_blk, cols, col_blk):
    return _iota((rows, cols), 0) // row_blk == _iota((rows, cols), 1) // col_blk


def _tile_bd(x16, mask):
    return jnp.where(mask, jnp.concatenate([x16] * RW_H, axis=0), jnp.zeros((), BF16))


def _rwkv_streams(z, prev, pr):
    ones_bd = jnp.where(_block_diag_mask(BW, RW_HD, BW, RW_HD), 1.0, 0.0).astype(BF16)
    zm = z + (prev - z) * pr["mu"][...]
    r = zm[:, 0:BW]
    k = zm[:, BW:2 * BW]
    v = zm[:, 2 * BW:3 * BW]
    lo = zm[:, 3 * BW:]
    xw = -(pr["w0"][...] + _dot(jnp.tanh(lo).astype(BF16), pr["w2"][...]))
    softplus = jnp.maximum(xw, 0.0) + jnp.log(1.0 + jnp.exp(-jnp.abs(xw)))
    logw = -jnp.exp(-softplus - 0.5)
    a_sig = _sigmoid(pr["a0"][...] + _dot(lo.astype(BF16), pr["a2"][...]))
    g = _dot(_sigmoid(lo).astype(BF16), pr["g2"][...])
    kk = k * pr["k_k"][...]
    kk = kk / jnp.maximum(jnp.sqrt(_dot_x01(kk * kk, ones_bd)), 1e-12)
    k = k * (1.0 + (a_sig - 1.0) * pr["k_a"][...])
    bonus = _dot_x01(r * k * pr["r_k"][...], ones_bd) * v
    return r, k, v, logw, -kk, kk * a_sig, g, bonus


def _rwkv_chunk_prep(r, k, v, logw, a, bm, c):
    cw = RW_H * c
    bd_rows = _block_diag_mask(cw, c, BW, RW_HD)
    bd_sq = _block_diag_mask(cw, c, cw, c)
    tt = _iota((c, cw), 0)
    ss = _iota((c, cw), 1) % c
    tri = jnp.where(_iota((c, c), 1) <= _iota((c, c), 0), 1.0, 0.0).astype(BF16)
    cl = _dot_01x(tri, logw)
    cl_last = cl[c - 1:c, :]
    e_out = jnp.exp(-cl)
    e_end = jnp.exp(cl_last - cl)
    ar = jnp.concatenate([a * jnp.exp(cl - logw), r * jnp.exp(cl)], axis=0).astype(BF16)
    kb_bd = jnp.concatenate([_tile_bd((k * e_out).astype(BF16), bd_rows),
                             _tile_bd((bm * e_out).astype(BF16), bd_rows)], axis=0)
    gm = _dot_nt(ar, kb_bd)
    l_ak = jnp.where(ss < tt, gm[0:c, 0:cw], 0.0)
    l_ab = jnp.where(ss < tt, gm[0:c, cw:2 * cw], 0.0)
    m_rk = jnp.where(ss <= tt, gm[c:2 * c, 0:cw], 0.0)
    m_rb = jnp.where(ss <= tt, gm[c:2 * c, cw:2 * cw], 0.0)
    lv = _dot(jnp.concatenate([l_ak, m_rk], axis=0).astype(BF16), _tile_bd(v.astype(BF16), bd_rows))
    p, q = l_ab, l_ab
    for _ in range(max(1, (c - 1).bit_length())):
        pq = _dot(jnp.concatenate([p, q], axis=0).astype(BF16), _tile_bd(q.astype(BF16), bd_sq))
        p = p + pq[0:c]
        q = pq[c:2 * c]
    pm = jnp.concatenate([p, m_rb], axis=0).astype(BF16)
    kb_end = jnp.concatenate([k * e_end, bm * e_end], axis=0).astype(BF16)
    return ar, lv, pm, kb_end, jnp.exp(cl_last)


def _rwkv_chunk_apply(s_big, ar, lv, pm, kb_end, decay, v16, c):
    bd_rows = _block_diag_mask(RW_H * c, c, BW, RW_HD)
    xs = _dot_nt(ar, s_big.astype(BF16))
    x0 = xs[0:c] + lv[0:c]
    sa16 = (x0 + _dot(pm[0:c], _tile_bd(x0.astype(BF16), bd_rows))).astype(BF16)
    s_new = s_big * decay + _dot_tn(jnp.concatenate([v16, sa16], axis=0), kb_end)
    y = xs[c:2 * c] + lv[c:2 * c] + _dot(pm[c:2 * c], _tile_bd(sa16, bd_rows))
    return y, jnp.where(_block_diag_mask(BW, RW_HD, BW, RW_HD), s_new, 0.0)


def _rwkv_post(y, g, bonus_g, lng, lnb):
    ones_bd = jnp.where(_block_diag_mask(BW, RW_HD, BW, RW_HD), 1.0, 0.0).astype(BF16)
    mean = _dot_x01(y, ones_bd) * (1.0 / RW_HD)
    yc = y - mean
    var = _dot_x01(yc * yc, ones_bd) * (1.0 / RW_HD)
    return (yc * lax.rsqrt(var + GN_EPS) * lng + lnb) * g + bonus_g


_RW_PARAMS = ("mu", "w0", "w2", "a0", "a2", "g2", "k_k", "k_a", "r_k")


def _rw_param_specs(l):
    vec = lambda w: _layer_spec((1, w), l)
    lora = lambda: _layer_spec((128, BW), l)
    return [vec(RW_IN), vec(BW), lora(), vec(BW), lora(), lora(), vec(BW), vec(BW), vec(BW)]


def _rwkv_fused_kernel(z_ref, sh0_ref, s0_ref, *refs, bb, c):
    pr = dict(zip(_RW_PARAMS, refs[:len(_RW_PARAMS)]))
    lng_ref, lnb_ref, y_ref, s_ref, zb_ref = refs[len(_RW_PARAMS):]
    z = z_ref[...]
    for b in range(bb):
        zb_ref[b, 7:8, :] = sh0_ref[b]
        zb_ref[b, 8:8 + c, :] = z[b * c:(b + 1) * c, :]
    prev = jnp.concatenate([zb_ref[b, 7:7 + c, :] for b in range(bb)], axis=0)
    r, k, v, logw, a, bm, g, bonus = _rwkv_streams(z, prev, pr)
    spread = jnp.where(_iota((RW_HD, BW), 1) % RW_HD == _iota((RW_HD, BW), 0), 1.0, 0.0).astype(BF16)
    gather = jnp.where(_iota((BW, RW_HD), 0) % RW_HD == _iota((BW, RW_HD), 1), 1.0, 0.0).astype(BF16)
    bd_state = _block_diag_mask(BW, RW_HD, BW, RW_HD)
    ys = []
    for b in range(bb):
        sl = slice(b * c, (b + 1) * c)
        ar, lv, pm, kb_end, decay = _rwkv_chunk_prep(r[sl], k[sl], v[sl], logw[sl], a[sl], bm[sl], c)
        s_big = jnp.where(bd_state, _dot_x01(s0_ref[b], spread), 0.0)
        y, s_new = _rwkv_chunk_apply(s_big, ar, lv, pm, kb_end, decay, v[sl].astype(BF16), c)
        s_ref[b] = _dot_x01(s_new, gather)
        ys.append(y)
    y_ref[...] = _rwkv_post(jnp.concatenate(ys, axis=0), g, bonus * g, lng_ref[...], lnb_ref[...])


def _rwkv_fused(zr, row0, bsz, c, shift0, s0, rp, l, bb):
    blk0 = row0 // (bb * c)
    vec = lambda w: _layer_spec((1, w), l)
    return pl.pallas_call(
        functools.partial(_rwkv_fused_kernel, bb=bb, c=c),
        out_shape=(jax.ShapeDtypeStruct((bsz * c, BW), F32), jax.ShapeDtypeStruct((bsz, BW, RW_HD), F32)),
        grid=(bsz // bb,),
        in_specs=[pl.BlockSpec((bb * c, RW_IN), lambda i: (blk0 + i, 0)),
                  pl.BlockSpec((bb, 1, RW_IN), lambda i: (i, 0, 0)),
                  pl.BlockSpec((bb, BW, RW_HD), lambda i: (i, 0, 0))] + _rw_param_specs(l) + [vec(BW), vec(BW)],
        out_specs=(pl.BlockSpec((bb * c, BW), lambda i: (i, 0)),
                   pl.BlockSpec((bb, BW, RW_HD), lambda i: (i, 0, 0))),
        scratch_shapes=[pltpu.VMEM((bb, 8 + c, RW_IN), F32)],
        compiler_params=_params(("parallel",)),
        name="rwkv7_fused",
    )(zr, shift0, s0, *[rp[n] for n in _RW_PARAMS], rp["ln_g"], rp["ln_b"])


def _rwkv_prep_kernel(*refs, bsz, nc, c):
    z_refs, tail_refs = refs[0:bsz], refs[bsz:2 * bsz]
    sh0_ref = refs[2 * bsz]
    pr = dict(zip(_RW_PARAMS, refs[2 * bsz + 1:2 * bsz + 1 + len(_RW_PARAMS)]))
    ar_ref, lv_ref, pm_ref, kbe_ref, dec_ref, v_ref, g_ref, bg_ref, zb_ref = refs[2 * bsz + 1 + len(_RW_PARAMS):]
    first = pl.program_id(0) == 0
    rows = nc * c
    for b in range(bsz):
        z = z_refs[b][...]
        zb_ref[7:8, :] = jnp.where(first, sh0_ref[b], tail_refs[b][7:8, :])
        zb_ref[8:8 + rows, :] = z
        r, k, v, logw, a, bm, g, bonus = _rwkv_streams(z, zb_ref[7:7 + rows, :], pr)
        v_ref[b] = v.astype(BF16)
        g_ref[b] = g
        bg_ref[b] = bonus * g
        for n in range(nc):
            sl = slice(n * c, (n + 1) * c)
            ar, lv, pm, kb_end, decay = _rwkv_chunk_prep(r[sl], k[sl], v[sl], logw[sl], a[sl], bm[sl], c)
            ar_ref[b, n] = ar
            lv_ref[b, n] = lv
            pm_ref[b, n] = pm
            kbe_ref[b, n] = kb_end
            dec_ref[b, n] = decay


def _rwkv_apply_kernel(ar_ref, lv_ref, pm_ref, kbe_ref, dec_ref, v_ref, s0_ref, y_ref, s_ref, *, bsz, c):
    @pl.when(pl.program_id(0) == 0)
    def _():
        s_ref[...] = s0_ref[...]

    for b in range(bsz):
        y, s_new = _rwkv_chunk_apply(s_ref[b], ar_ref[b, 0], lv_ref[b, 0], pm_ref[b, 0], kbe_ref[b, 0],
                                     dec_ref[b, 0], v_ref[b], c)
        s_ref[b] = s_new
        y_ref[b] = y


def _rwkv_post_kernel(y_ref, g_ref, bg_ref, lng_ref, lnb_ref, o_ref):
    o_ref[...] = _rwkv_post(y_ref[...], g_ref[...], bg_ref[...], lng_ref[...], lnb_ref[...])


def _rwkv_long(zr, bsz, t, shift0, s0, rp, l, c, nc):
    rows = nc * c
    n_ch, n_steps = t // c, t // rows
    vec = lambda w: _layer_spec((1, w), l)
    z_specs = [pl.BlockSpec((rows, RW_IN), lambda j, b=b: (b * n_steps + j, 0)) for b in range(bsz)]
    tail_specs = [pl.BlockSpec((8, RW_IN), lambda j, b=b: (jnp.maximum((b * t + j * rows) // 8 - 1, 0), 0))
                  for b in range(bsz)]
    chunked = lambda h, w, dt: jax.ShapeDtypeStruct((bsz, n_ch, h, w), dt)
    chunk_out = lambda h, w: pl.BlockSpec((bsz, nc, h, w), lambda j: (0, j, 0, 0))
    tok_out = lambda: pl.BlockSpec((bsz, rows, BW), lambda j: (0, j, 0))
    ar, lv, pm, kbe, dec, v16, g, bg = pl.pallas_call(
        functools.partial(_rwkv_prep_kernel, bsz=bsz, nc=nc, c=c),
        out_shape=(chunked(2 * c, BW, BF16), chunked(2 * c, BW, F32), chunked(2 * c, RW_H * c, BF16),
                   chunked(2 * c, BW, BF16), chunked(1, BW, F32),
                   jax.ShapeDtypeStruct((bsz, t, BW), BF16), jax.ShapeDtypeStruct((bsz, t, BW), F32),
                   jax.ShapeDtypeStruct((bsz, t, BW), F32)),
        grid=(n_steps,),
        in_specs=z_specs + tail_specs + [_const_spec((bsz, 1, RW_IN))] + _rw_param_specs(l),
        out_specs=(chunk_out(2 * c, BW), chunk_out(2 * c, BW), chunk_out(2 * c, RW_H * c), chunk_out(2 * c, BW),
                   chunk_out(1, BW), tok_out(), tok_out(), tok_out()),
        scratch_shapes=[pltpu.VMEM((8 + rows, RW_IN), F32)],
        compiler_params=_params(("parallel",)),
        name="rwkv7_prep",
    )(*([zr] * (2 * bsz)), shift0, *[rp[n] for n in _RW_PARAMS])
    chunk_in = lambda h, w: pl.BlockSpec((bsz, 1, h, w), lambda j: (0, j, 0, 0))
    y_raw, s_new = pl.pallas_call(
        functools.partial(_rwkv_apply_kernel, bsz=bsz, c=c),
        out_shape=(jax.ShapeDtypeStruct((bsz, t, BW), F32), jax.ShapeDtypeStruct((bsz, BW, BW), F32)),
        grid=(n_ch,),
        in_specs=[chunk_in(2 * c, BW), chunk_in(2 * c, BW), chunk_in(2 * c, RW_H * c), chunk_in(2 * c, BW),
                  chunk_in(1, BW), pl.BlockSpec((bsz, c, BW), lambda j: (0, j, 0)), _const_spec((bsz, BW, BW))],
        out_specs=(pl.BlockSpec((bsz, c, BW), lambda j: (0, j, 0)), _const_spec((bsz, BW, BW))),
        compiler_params=_params(("arbitrary",)),
        name="rwkv7_apply",
    )(ar, lv, pm, kbe, dec, v16, s0)
    n = bsz * t
    tm = _pick(n, (512, 256, 128, 64))
    row = pl.BlockSpec((tm, BW), lambda i: (i, 0))
    y = pl.pallas_call(
        _rwkv_post_kernel,
        out_shape=jax.ShapeDtypeStruct((n, BW), F32),
        grid=(n // tm,),
        in_specs=[row, row, row, vec(BW), vec(BW)],
        out_specs=row,
        compiler_params=_params(("parallel",)),
        name="rwkv7_post",
    )(y_raw.reshape(n, BW), g.reshape(n, BW), bg.reshape(n, BW), rp["ln_g"], rp["ln_b"])
    return y, s_new


def _conv_kernel(u_ref, buf_ref, w_ref, cb_ref, g_ref, b_ref, y_ref, nb_ref, xp_ref, *, tt):
    j = pl.program_id(1)
    halo = CONV_K - 1
    top = 32 - halo

    @pl.when(j == 0)
    def _():
        xp_ref[:, top:32, :] = buf_ref[...]

    xp_ref[:, 32:32 + tt, :] = u_ref[...].reshape(xp_ref.shape[0], tt, BW)
    acc = xp_ref[:, top:top + tt, :] * w_ref[0:1, :] + cb_ref[...]
    for k in range(1, CONV_K):
        acc = acc + xp_ref[:, top + k:top + k + tt, :] * w_ref[k:k + 1, :]
    yn = _layer_norm(acc, g_ref[...], b_ref[...])
    y_ref[...] = yn * _sigmoid(yn)
    new = xp_ref[:, top + tt:32 + tt, :]
    xp_ref[:, top:32, :] = new
    nb_ref[...] = new


def _conv(u, row0, bsz, t, buf, cp, l, bb, tt):
    assert bb == 1 or tt == t
    halo = CONV_K - 1
    nt = t // tt
    blk0 = row0 // (bb * tt)
    vec = lambda: _layer_spec((1, BW), l)
    return pl.pallas_call(
        functools.partial(_conv_kernel, tt=tt),
        out_shape=(jax.ShapeDtypeStruct((bsz, t, BW), F32), jax.ShapeDtypeStruct((bsz, halo, BW), F32)),
        grid=(bsz // bb, t // tt),
        in_specs=[pl.BlockSpec((bb * tt, BW), lambda i, j: (blk0 + i * nt + j, 0)),
                  pl.BlockSpec((bb, halo, BW), lambda i, j: (i, 0, 0)),
                  _layer_spec((CONV_K, BW), l), vec(), vec(), vec()],
        out_specs=(pl.BlockSpec((bb, tt, BW), lambda i, j: (i, j, 0)),
                   pl.BlockSpec((bb, halo, BW), lambda i, j: (i, 0, 0))),
        scratch_shapes=[pltpu.VMEM((bb, 32 + tt, BW), F32)],
        compiler_params=_params(("parallel", "arbitrary")),
        name="conv_module",
    )(u, buf, cp["w"], cp["b"], cp["ln_g"], cp["ln_b"])


def _gelu_tanh(x):
    return 0.5 * x * (1.0 + jnp.tanh(math.sqrt(2.0 / math.pi) * (x + 0.044715 * (x * x * x))))


def _s5_out(u, hre, him, cre_ref, cim_ref, d_ref, gw_ref, gb_ref):
    y = _dot(hre.astype(BF16), cre_ref[...]) - _dot(him.astype(BF16), cim_ref[...]) + d_ref[...] * u
    y = _gelu_tanh(y)
    return y * _sigmoid(_dot(y.astype(BF16), gw_ref[...]) + gb_ref[...])


def _s5_prompt_kernel(u_ref, bre_ref, bim_ref, pre_ref, pim_ref, cre_ref, cim_ref, d_ref, gw_ref, gb_ref,
                      y_ref, hre_ref, him_ref, xre_ref, xim_ref, *, tt):
    j = pl.program_id(1)

    @pl.when(j == 0)
    def _():
        hre_ref[...] = jnp.zeros_like(hre_ref)
        him_ref[...] = jnp.zeros_like(him_ref)

    u = u_ref[...]
    ub = u.astype(BF16)
    xre = _dot(ub, bre_ref[...])
    xim = _dot(ub, bim_ref[...])
    xre = xre.reshape(tt // 8, 8, S5_N)
    xim = xim.reshape(tt // 8, 8, S5_N)
    for s in range(3):
        d = 1 << s
        are = pre_ref[s]
        aim = pim_ref[s]
        sre = pltpu.roll(xre, d, 1)
        sim = pltpu.roll(xim, d, 1)
        xre, xim = xre + (are * sre - aim * sim), xim + (are * sim + aim * sre)
    xre_ref[...] = xre.reshape(tt, S5_N)
    xim_ref[...] = xim.reshape(tt, S5_N)
    t8re = pre_ref[3]
    t8im = pim_ref[3]

    def group(g, carry):
        cre, cim = carry
        rows = pl.ds(pl.multiple_of(g * 8, 8), 8)
        gre = xre_ref[rows, :] + (t8re * cre - t8im * cim)
        gim = xim_ref[rows, :] + (t8re * cim + t8im * cre)
        xre_ref[rows, :] = gre
        xim_ref[rows, :] = gim
        return (jnp.broadcast_to(gre[7:8, :], (8, S5_N)), jnp.broadcast_to(gim[7:8, :], (8, S5_N)))

    cre, cim = lax.fori_loop(0, tt // 8, group, (jnp.broadcast_to(hre_ref[...], (8, S5_N)),
                                                 jnp.broadcast_to(him_ref[...], (8, S5_N))))
    hre_ref[...] = cre[0:1, :]
    him_ref[...] = cim[0:1, :]
    y_ref[...] = _s5_out(u, xre_ref[...], xim_ref[...], cre_ref, cim_ref, d_ref, gw_ref, gb_ref)


def _s5_prompt(u, bsz, t, sp, l, tt):
    nt = t // tt
    return pl.pallas_call(
        functools.partial(_s5_prompt_kernel, tt=tt),
        out_shape=(jax.ShapeDtypeStruct((bsz, t, BW), F32),
                   jax.ShapeDtypeStruct((bsz, 1, S5_N), F32), jax.ShapeDtypeStruct((bsz, 1, S5_N), F32)),
        grid=(bsz, nt),
        in_specs=[pl.BlockSpec((tt, BW), lambda i, j: (i * nt + j, 0)),
                  _layer_spec((BW, S5_N), l), _layer_spec((BW, S5_N), l),
                  _layer_spec((4, 8, S5_N), l), _layer_spec((4, 8, S5_N), l),
                  _layer_spec((S5_N, BW), l), _layer_spec((S5_N, BW), l),
                  _layer_spec((1, BW), l), _layer_spec((BW, BW), l), _layer_spec((1, BW), l)],
        out_specs=(pl.BlockSpec((None, tt, BW), lambda i, j: (i, j, 0)),
                   pl.BlockSpec((None, 1, S5_N), lambda i, j: (i, 0, 0)),
                   pl.BlockSpec((None, 1, S5_N), lambda i, j: (i, 0, 0))),
        scratch_shapes=[pltpu.VMEM((tt, S5_N), F32), pltpu.VMEM((tt, S5_N), F32)],
        compiler_params=_params(("parallel", "arbitrary")),
        name="s5_prompt",
    )(u, sp["bre"], sp["bim"], sp["pre"], sp["pim"], sp["cre"], sp["cim"], sp["d"], sp["glu_w"], sp["glu_b"])


def _s5_sample_kernel(u_ref, h0re_ref, h0im_ref, bre_ref, bim_ref, pre_ref, pim_ref, cre_ref, cim_ref,
                      d_ref, gw_ref, gb_ref, y_ref, hre_ref, him_ref, *, ts):
    lre = pre_ref[3][0:1, :]
    lim = pim_ref[3][0:1, :]
    hre = h0re_ref[...]
    him = h0im_ref[...]
    for t in range(ts):
        u = u_ref[t]
        ub = u.astype(BF16)
        hre, him = (lre * hre - lim * him + _dot(ub, bre_ref[...]),
                    lre * him + lim * hre + _dot(ub, bim_ref[...]))
        y_ref[t] = _s5_out(u, hre, him, cre_ref, cim_ref, d_ref, gw_ref, gb_ref)
    hre_ref[...] = hre
    him_ref[...] = him


def _s5_sample(u_tm, h0re, h0im, sp, l):
    ts, bsz, _ = u_tm.shape
    return pl.pallas_call(
        functools.partial(_s5_sample_kernel, ts=ts),
        out_shape=(jax.ShapeDtypeStruct((ts, bsz, BW), F32),
                   jax.ShapeDtypeStruct((bsz, S5_N), F32), jax.ShapeDtypeStruct((bsz, S5_N), F32)),
        grid=(1,),
        in_specs=[_const_spec((ts, bsz, BW)), _const_spec((bsz, S5_N)), _const_spec((bsz, S5_N)),
                  _layer_spec((BW, S5_N), l), _layer_spec((BW, S5_N), l),
                  _layer_spec((4, 8, S5_N), l), _layer_spec((4, 8, S5_N), l),
                  _layer_spec((S5_N, BW), l), _layer_spec((S5_N, BW), l),
                  _layer_spec((1, BW), l), _layer_spec((BW, BW), l), _layer_spec((1, BW), l)],
        out_specs=(_const_spec((ts, bsz, BW)), _const_spec((bsz, S5_N)), _const_spec((bsz, S5_N))),
        compiler_params=_params(("arbitrary",)),
        name="s5_sample",
    )(u_tm, h0re, h0im, sp["bre"], sp["bim"], sp["pre"], sp["pim"], sp["cre"], sp["cim"], sp["d"],
      sp["glu_w"], sp["glu_b"])


def _merge_kernel(h_ref, *refs, p_tiles):
    yp_refs, ys_refs = refs[0:N_BRANCH], refs[N_BRANCH:2 * N_BRANCH]
    wg_ref, bg_ref, wbr_ref, wo_ref, g_ref, b_ref, o_ref = refs[2 * N_BRANCH:]
    is_prompt = pl.program_id(0) < p_tiles
    h = h_ref[...]
    hb = h.astype(BF16)
    m = None
    for n in range(N_BRANCH):
        y = jnp.where(is_prompt, yp_refs[n][...], ys_refs[n][...])
        gate = _sigmoid(_dot(hb, wg_ref[:, n * D_MODEL:(n + 1) * D_MODEL]) + bg_ref[n:n + 1, :])
        term = gate * _dot(y.astype(BF16), wbr_ref[n])
        m = term if m is None else m + term
    out = _dot(m.astype(BF16), wo_ref[...])
    o_ref[...] = _layer_norm(ALPHA * h + out, g_ref[...], b_ref[...])


def _merge(h, ys_p, ys_s, mw, l):
    n = h.shape[0]
    n_p, n_s = ys_p[0].shape[0], ys_s[0].shape[0]
    tm = _pick(math.gcd(n_p, n_s), (512, 256, 128, 64, 32, 16, 8))
    p_tiles, s_tiles = n_p // tm, n_s // tm
    row = lambda w: pl.BlockSpec((tm, w), lambda i: (i, 0))
    p_row = pl.BlockSpec((tm, BW), lambda i: (jnp.minimum(i, p_tiles - 1), 0))
    s_row = pl.BlockSpec((tm, BW), lambda i: (jnp.clip(i - p_tiles, 0, s_tiles - 1), 0))
    return pl.pallas_call(
        functools.partial(_merge_kernel, p_tiles=p_tiles),
        out_shape=jax.ShapeDtypeStruct((n, D_MODEL), F32),
        grid=(n // tm,),
        in_specs=[row(D_MODEL)] + [p_row] * N_BRANCH + [s_row] * N_BRANCH + [
                  _layer_spec((D_MODEL, N_BRANCH * D_MODEL), l), _layer_spec((N_BRANCH, D_MODEL), l),
                  _layer_spec((N_BRANCH, BW, D_MODEL), l), _layer_spec((D_MODEL, D_MODEL), l),
                  _layer_spec((1, D_MODEL), l), _layer_spec((1, D_MODEL), l)],
        out_specs=row(D_MODEL),
        compiler_params=_params(("parallel",)),
        name="branch_merge",
    )(h, *ys_p, *ys_s, mw["wg"], mw["bg"], mw["wbr"], mw["wo"], mw["g"], mw["b"])


def _pad_rows(w, lo, total):
    return jnp.pad(w, ((0, 0), (lo, total - lo - w.shape[1]), (0, 0)))


def _s5_params(a_re, a_im, log_dt, b_re, b_im, c_re, c_im):
    dep = a_re.shape[0]
    dt = jnp.exp(log_dt)[..., None]
    mag = jnp.exp(a_re * dt)
    lam_re, lam_im = mag * jnp.cos(a_im * dt), mag * jnp.sin(a_im * dt)
    den = a_re * a_re + a_im * a_im
    co_re = ((lam_re - 1.0) * a_re + lam_im * a_im) / den
    co_im = (lam_im * a_re - (lam_re - 1.0) * a_im) / den
    bb_re = co_re[..., None] * b_re - co_im[..., None] * b_im
    bb_im = co_re[..., None] * b_im + co_im[..., None] * b_re
    eye = jnp.eye(S5_GROUPS, dtype=F32)
    def bmat(x):
        return jnp.einsum("lgpc,gk->lgckp", x, eye).reshape(dep, BW, S5_N)
    def cmat(x):
        return jnp.einsum("lgcp,gk->lgpkc", x, eye).reshape(dep, S5_N, BW)
    lr, li = lam_re.reshape(dep, 1, S5_N), lam_im.reshape(dep, 1, S5_N)
    row = jnp.arange(8)[None, :, None]
    pre, pim = [], []
    xr, xi = lr, li
    for s in range(3):
        pre.append(jnp.where(row >= (1 << s), xr, 0.0))
        pim.append(jnp.where(row >= (1 << s), xi, 0.0))
        xr, xi = xr * xr - xi * xi, 2.0 * xr * xi
    seq_r, seq_i = [lr], [li]
    for _ in range(7):
        xr, xi = seq_r[-1], seq_i[-1]
        seq_r.append(xr * lr - xi * li)
        seq_i.append(xr * li + xi * lr)
    pre.append(jnp.concatenate(seq_r, axis=1))
    pim.append(jnp.concatenate(seq_i, axis=1))
    return dict(bre=bmat(bb_re).astype(BF16), bim=bmat(bb_im).astype(BF16),
                pre=jnp.stack(pre, axis=1), pim=jnp.stack(pim, axis=1),
                cre=cmat(c_re).astype(BF16), cim=cmat(c_im).astype(BF16))


def _rope_tables(pos):
    inv = 1.0 / (ROPE_BASE ** (jnp.arange(0, MLA_ROPE, 2, dtype=F32) / MLA_ROPE))
    ang = pos.astype(F32)[:, None] * inv[None, :]
    cos, sin = jnp.cos(ang), jnp.sin(ang)
    return jnp.concatenate([cos, cos], -1), jnp.concatenate([-sin, sin], -1)


def kernel(x_prompt, x_sample, cache_ckv, cache_kpe, state_rwkv, state_rwkv_shift, state_conv, state_s5_re, state_s5_im, page_table, ffn1_w_in, ffn1_w_down, ln1_g, ln1_b, w_in, b_gate, mla_q_norm, mla_w_uq, mla_kv_norm, mla_w_uk, mla_w_uv, rwkv_mu, rwkv_w0, rwkv_w2, rwkv_a0, rwkv_a2, rwkv_g2, rwkv_k_k, rwkv_k_a, rwkv_r_k, rwkv_ln_g, rwkv_ln_b, conv_w, conv_b, conv_ln_g, conv_ln_b, s5_a_re, s5_a_im, s5_log_dt, s5_b_re, s5_b_im, s5_c_re, s5_c_im, s5_d, s5_glu_w, s5_glu_b, w_branch, w_out, ln2_g, ln2_b, ffn2_w_in, ffn2_w_down, ln3_g, ln3_b):
    bp, tp, _ = x_prompt.shape
    bs, ts, _ = x_sample.shape
    depth = w_in.shape[0]
    n_p, n_s = bp * tp, bs * ts
    past = page_table.shape[1] * PAGE
    halo = CONV_K - 1
    vec = lambda w: w[:, None, :]

    o_kv = MLA_Q
    o_kp = o_kv + MLA_KV
    o_r = o_kp + MLA_ROPE
    o_c = o_r + RW_IN
    o_s = o_c + 2 * BW
    o_g = o_s + BW
    half = MLA_ROPE // 2
    wkp = w_in[:, :, o_kp:o_r]
    uq = mla_w_uq.reshape(depth, MLA_Q, MLA_H, MLA_NOPE + MLA_ROPE)
    uq_pe = uq[..., MLA_NOPE:]
    swap = lambda x: jnp.concatenate([x[..., half:], x[..., :half]], -1)
    to_heads = lambda x: jnp.moveaxis(x, 2, 1).astype(BF16)
    pw = dict(wq=w_in[:, :, :o_kv].astype(BF16), wkv=w_in[:, :, o_kv:o_kp].astype(BF16),
              wkp=wkp.astype(BF16), wkps=swap(wkp).astype(BF16),
              wr=w_in[:, :, o_r:o_c].astype(BF16), wc=w_in[:, :, o_c:o_s].astype(BF16),
              ws=w_in[:, :, o_s:o_g].astype(BF16),
              gq=vec(mla_q_norm), gkv=vec(mla_kv_norm),
              wn=to_heads(uq[..., :MLA_NOPE]), wp=to_heads(uq_pe), wps=to_heads(swap(uq_pe)),
              wuk=jnp.transpose(mla_w_uk, (0, 2, 3, 1)).astype(BF16))
    eye_h = jnp.eye(MLA_H, dtype=F32)
    wuv = jnp.einsum("lrhv,hk->lhrkv", mla_w_uv, eye_h).reshape(depth, MLA_H, MLA_KV, BW).astype(BF16)
    rp = dict(mu=vec(rwkv_mu), w0=vec(rwkv_w0), a0=vec(rwkv_a0),
              w2=_pad_rows(rwkv_w2, 0, 128).astype(BF16), a2=_pad_rows(rwkv_a2, 32, 128).astype(BF16),
              g2=_pad_rows(rwkv_g2, 64, 128).astype(BF16),
              k_k=vec(rwkv_k_k), k_a=vec(rwkv_k_a), r_k=vec(rwkv_r_k.reshape(depth, BW)),
              ln_g=vec(rwkv_ln_g), ln_b=vec(rwkv_ln_b))
    cp = dict(w=conv_w, b=vec(conv_b), ln_g=vec(conv_ln_g), ln_b=vec(conv_ln_b))
    tt_s5 = _pick(tp, (512, 256, 128))
    sp = _s5_params(s5_a_re, s5_a_im, s5_log_dt, s5_b_re, s5_b_im, s5_c_re, s5_c_im)
    sp.update(d=vec(s5_d), glu_w=s5_glu_w.astype(BF16), glu_b=vec(s5_glu_b))
    mw = dict(wg=w_in[:, :, o_g:].astype(BF16), bg=b_gate, wbr=w_branch.astype(BF16), wo=w_out.astype(BF16),
              g=vec(ln2_g), b=vec(ln2_b))
    f1 = (ffn1_w_in.astype(BF16), ffn1_w_down.astype(BF16), vec(ln1_g), vec(ln1_b))
    f2 = (ffn2_w_in.astype(BF16), ffn2_w_down.astype(BF16), vec(ln3_g), vec(ln3_b))

    cos_p, sin_p = _rope_tables(jnp.arange(tp))
    cos_s, sin_s = _rope_tables(past + jnp.arange(ts))
    cos2 = jnp.concatenate([jnp.tile(cos_p, (bp, 1)), jnp.tile(cos_s, (bs, 1))], 0)
    sin2 = jnp.concatenate([jnp.tile(sin_p, (bp, 1)), jnp.tile(sin_s, (bs, 1))], 0)

    c_rw = _pick(tp, (64, 32, 16, 8))
    nc_rw = _pick(tp // c_rw, (2, 1))
    bb_rw = _pick(bs, (4, 2, 1))
    tt_cv = _pick(tp, (512, 256, 128, 64, 32))
    bb_cv = _pick(bs, (32, 16, 8, 4, 2, 1))

    cache_kpe_t = jnp.swapaxes(cache_kpe, 2, 3)
    x = jnp.concatenate([x_prompt.reshape(n_p, D_MODEL), x_sample.reshape(n_s, D_MODEL)], 0)
    outs = [[] for _ in range(14)]
    for l in range(depth):
        h = _ffn_ln(x, *f1, l)
        qcat, q16, ckv, kpe, kcat, zr, u, zs = _prep(h, cos2, sin2, pw, l)
        ya_p = _attn_prompt(q16, kcat, wuv, l, bp, tp)
        ya_s = _attn_sample(page_table, qcat, ckv, kpe, cache_ckv, cache_kpe_t, wuv, l, n_p, bs, ts)
        yb_p, srw_p = _rwkv_long(zr, bp, tp, jnp.zeros((bp, 1, RW_IN), F32), jnp.zeros((bp, BW, BW), F32),
                                 rp, l, c_rw, nc_rw)
        yb_s, srw_s = _rwkv_fused(zr, n_p, bs, ts, state_rwkv_shift[l][:, None, :],
                                  state_rwkv[l].reshape(bs, BW, RW_HD), rp, l, bb_rw)
        yc_p, cv_p = _conv(u, 0, bp, tp, jnp.zeros((bp, halo, BW), F32), cp, l, 1, tt_cv)
        yc_s, cv_s = _conv(u, n_p, bs, ts, state_conv[l], cp, l, bb_cv, ts)
        yd_p, s5re_p, s5im_p = _s5_prompt(zs, bp, tp, sp, l, tt_s5)
        yd_s, s5re_s, s5im_s = _s5_sample(jnp.swapaxes(zs[n_p:].reshape(bs, ts, BW), 0, 1),
                                          state_s5_re[l].reshape(bs, S5_N), state_s5_im[l].reshape(bs, S5_N), sp, l)
        yd_s = jnp.swapaxes(yd_s, 0, 1)
        flat = lambda a: a.reshape(-1, BW)
        x = _merge(h, (ya_p, flat(yb_p), flat(yc_p), flat(yd_p)), (ya_s, flat(yb_s), flat(yc_s), flat(yd_s)),
                   mw, l)
        x = _ffn_ln(x, *f2, l)
        zr_p = zr[:n_p].reshape(bp, tp, RW_IN)
        zr_s = zr[n_p:].reshape(bs, ts, RW_IN)

        diag = lambda s: jnp.stack([s[:, i * RW_HD:(i + 1) * RW_HD, i * RW_HD:(i + 1) * RW_HD]
                                    for i in range(RW_H)], 1)
        new_p = (ckv[:n_p].reshape(bp, tp, MLA_KV), kpe[:n_p].reshape(bp, tp, MLA_ROPE), diag(srw_p),
                 zr_p[:, -1], cv_p, s5re_p.reshape(bp, S5_GROUPS, S5_P), s5im_p.reshape(bp, S5_GROUPS, S5_P))
        new_s = (ckv[n_p:].reshape(bs, ts, MLA_KV), kpe[n_p:].reshape(bs, ts, MLA_ROPE),
                 srw_s.reshape(bs, RW_H, RW_HD, RW_HD),
                 zr_s[:, -1], cv_s, s5re_s.reshape(bs, S5_GROUPS, S5_P), s5im_s.reshape(bs, S5_GROUPS, S5_P))
        for i, a in enumerate(new_p + new_s):
            outs[i].append(a)
    return (x[:n_p].reshape(bp, tp, D_MODEL), x[n_p:].reshape(bs, ts, D_MODEL)) + tuple(jnp.stack(o) for o in outs)
```

```python
import functools
import math

import jax
import jax.numpy as jnp
from jax import lax
from jax.experimental import pallas as pl
from jax.experimental.pallas import tpu as pltpu

F32 = jnp.float32
BF16 = jnp.bfloat16

D_MODEL = 1024
PAGE = 128
BW = D_MODEL // 4
N_BRANCH = 4
MLA_V = 64
MLA_H = BW // MLA_V
MLA_NOPE = 64
MLA_ROPE = 32
MLA_Q = D_MODEL // 4
MLA_KV = D_MODEL // 8
QK = MLA_KV + MLA_ROPE
KW = 256
ROPE_BASE = 10000.0
RW_HD = 64
RW_H = BW // RW_HD
RW_IN = 3 * BW + 128
CONV_K = 31
S5_G = 16
S5_GROUPS = BW // S5_G
S5_P = 64
S5_N = S5_GROUPS * S5_P
D_FF = 2816
FF_CHUNK = 256
DEPTH = 4
ALPHA = (2 * DEPTH) ** 0.25
ATTN_SCALE = (MLA_NOPE + MLA_ROPE) ** -0.5
LN_EPS = 1e-5
RMS_EPS = 1e-6
GN_EPS = 64e-5
VMEM_LIMIT = 56 * 1024 * 1024


def _pick(n, cands):
    for c in cands:
        if n % c == 0:
            return c
    raise ValueError(f"no tile in {cands} divides {n}")


def _const_spec(shape):
    nd = len(shape)
    return pl.BlockSpec(shape, lambda *_: (0,) * nd)


def _layer_spec(shape, l):
    nd = len(shape)
    return pl.BlockSpec((None,) + tuple(shape), lambda *_: (l,) + (0,) * nd)


def _params(sem, vmem=VMEM_LIMIT):
    return pltpu.CompilerParams(dimension_semantics=sem, vmem_limit_bytes=vmem)


def _dot(a, b, **kw):
    return jnp.dot(a, b, preferred_element_type=F32, **kw)


def _dot_nt(a, b, **kw):
    return lax.dot_general(a, b, (((1,), (1,)), ((), ())), preferred_element_type=F32, **kw)


def _dot_tn(a, b, **kw):
    return lax.dot_general(a, b, (((0,), (0,)), ((), ())), preferred_element_type=F32, **kw)


def _layer_norm(x, g, b):
    mu = jnp.mean(x, -1, keepdims=True)
    xc = x - mu
    var = jnp.mean(xc * xc, -1, keepdims=True)
    return xc * lax.rsqrt(var + LN_EPS) * g + b


def _rms_norm(x, g):
    return x * lax.rsqrt(jnp.mean(x * x, -1, keepdims=True) + RMS_EPS) * g


def _sigmoid(x):
    return 1.0 / (1.0 + jnp.exp(-x))


def _ffn_ln_kernel(x_ref, win_ref, wdn_ref, g_ref, b_ref, o_ref, acc_ref):
    x = x_ref[...]
    xb = x.astype(BF16)
    for c in range(D_FF // FF_CHUNK):
        lo = c * FF_CHUNK
        a = _dot(xb, win_ref[:, lo:lo + FF_CHUNK])
        b = _dot(xb, win_ref[:, D_FF + lo:D_FF + lo + FF_CHUNK])
        h = (a * _sigmoid(a) * b).astype(BF16)
        d = _dot(h, wdn_ref[lo:lo + FF_CHUNK, :])
        if c == 0:
            acc_ref[...] = d
        else:
            acc_ref[...] += d
    o_ref[...] = _layer_norm(ALPHA * x + 0.5 * acc_ref[...], g_ref[...], b_ref[...])


def _ffn_ln(x, w_in, w_down, g, b, l):
    n = x.shape[0]
    tm = _pick(n, (512, 256, 128, 64, 32, 16, 8))
    return pl.pallas_call(
        _ffn_ln_kernel,
        out_shape=jax.ShapeDtypeStruct((n, D_MODEL), F32),
        grid=(n // tm,),
        in_specs=[pl.BlockSpec((tm, D_MODEL), lambda i: (i, 0)),
                  _layer_spec((D_MODEL, 2 * D_FF), l), _layer_spec((D_FF, D_MODEL), l),
                  _layer_spec((1, D_MODEL), l), _layer_spec((1, D_MODEL), l)],
        out_specs=pl.BlockSpec((tm, D_MODEL), lambda i: (i, 0)),
        scratch_shapes=[pltpu.VMEM((tm, D_MODEL), F32)],
        compiler_params=_params(("parallel",)),
        name="ffn_ln",
    )(x, w_in, w_down, g, b)


def _prep_kernel(h_ref, cos_ref, sin_ref, wq_ref, wkv_ref, wkp_ref, wkps_ref, wr_ref, wc_ref, ws_ref,
                 gq_ref, gkv_ref, wn_ref, wp_ref, wps_ref, wuk_ref,
                 qcat_ref, q16_ref, ckv_ref, kpe_ref, kcat_ref, zr_ref, u_ref, zs_ref):
    hb = h_ref[...].astype(BF16)
    cos2 = cos_ref[...]
    sin2 = sin_ref[...]
    tm = hb.shape[0]
    zq = _rms_norm(_dot(hb, wq_ref[...]), gq_ref[...]).astype(BF16)
    hs = range(MLA_H)
    qn = [_dot(zq, wn_ref[h]).astype(BF16) for h in hs]
    qp = [_dot(zq, wp_ref[h]) for h in hs]
    qps = [_dot(zq, wps_ref[h]) for h in hs]
    ql = [_dot(qn[h], wuk_ref[h]) for h in hs]
    for h in hs:
        qpe = qp[h] * cos2 + qps[h] * sin2
        qcat_ref[h, :, 0:MLA_KV] = ql[h]
        qcat_ref[h, :, MLA_KV:QK] = qpe
        q16_ref[h, :, 0:MLA_KV] = ql[h].astype(BF16)
        q16_ref[h, :, MLA_KV:QK] = qpe.astype(BF16)
        q16_ref[h, :, QK:KW] = jnp.zeros((tm, KW - QK), BF16)
    ckv = _rms_norm(_dot(hb, wkv_ref[...]), gkv_ref[...])
    kpe = _dot(hb, wkp_ref[...]) * cos2 + _dot(hb, wkps_ref[...]) * sin2
    ckv_ref[...] = ckv
    kpe_ref[...] = kpe
    kcat_ref[:, 0:MLA_KV] = ckv.astype(BF16)
    kcat_ref[:, MLA_KV:QK] = kpe.astype(BF16)
    kcat_ref[:, QK:KW] = jnp.ones((tm, KW - QK), BF16)
    zr_ref[...] = _dot(hb, wr_ref[...])
    zc = _dot(hb, wc_ref[...])
    u_ref[...] = zc[:, :BW] * _sigmoid(zc[:, BW:])
    zs_ref[...] = _dot(hb, ws_ref[...])


def _prep(h, cos2, sin2, pw, l):
    n = h.shape[0]
    tm = _pick(n, (512, 256, 128, 64, 32, 16))
    row = lambda w: pl.BlockSpec((tm, w), lambda i: (i, 0))
    return pl.pallas_call(
        _prep_kernel,
        out_shape=(jax.ShapeDtypeStruct((MLA_H, n, QK), F32),
                   jax.ShapeDtypeStruct((MLA_H, n, KW), BF16),
                   jax.ShapeDtypeStruct((n, MLA_KV), F32),
                   jax.ShapeDtypeStruct((n, MLA_ROPE), F32),
                   jax.ShapeDtypeStruct((n, KW), BF16),
                   jax.ShapeDtypeStruct((n, RW_IN), F32),
                   jax.ShapeDtypeStruct((n, BW), F32),
                   jax.ShapeDtypeStruct((n, BW), F32)),
        grid=(n // tm,),
        in_specs=[row(D_MODEL), row(MLA_ROPE), row(MLA_ROPE),
                  _layer_spec((D_MODEL, MLA_Q), l), _layer_spec((D_MODEL, MLA_KV), l),
                  _layer_spec((D_MODEL, MLA_ROPE), l), _layer_spec((D_MODEL, MLA_ROPE), l),
                  _layer_spec((D_MODEL, RW_IN), l), _layer_spec((D_MODEL, 2 * BW), l),
                  _layer_spec((D_MODEL, BW), l),
                  _layer_spec((1, MLA_Q), l), _layer_spec((1, MLA_KV), l),
                  _layer_spec((MLA_H, MLA_Q, MLA_NOPE), l), _layer_spec((MLA_H, MLA_Q, MLA_ROPE), l),
                  _layer_spec((MLA_H, MLA_Q, MLA_ROPE), l), _layer_spec((MLA_H, MLA_NOPE, MLA_KV), l)],
        out_specs=(pl.BlockSpec((MLA_H, tm, QK), lambda i: (0, i, 0)),
                   pl.BlockSpec((MLA_H, tm, KW), lambda i: (0, i, 0)),
                   row(MLA_KV), row(MLA_ROPE), row(KW), row(RW_IN), row(BW), row(BW)),
        compiler_params=_params(("parallel",)),
        name="mixer_prep",
    )(h, cos2, sin2, pw["wq"], pw["wkv"], pw["wkp"], pw["wkps"], pw["wr"], pw["wc"], pw["ws"],
      pw["gq"], pw["gkv"], pw["wn"], pw["wp"], pw["wps"], pw["wuk"])


def _attn_prompt_kernel(q_ref, k_ref, wuv_ref, o_ref, m_ref, acc_ref, *, tq):
    i = pl.program_id(1)
    rep = lambda x, w: jnp.concatenate([x] * (w // 128), axis=1)
    m_ref[...] = jnp.full(m_ref.shape, -jnp.inf, F32)
    acc_ref[...] = jnp.zeros(acc_ref.shape, F32)

    def update(kb, diagonal):
        k = k_ref[pl.ds(pl.multiple_of(kb * tq, tq), tq), :]
        hs = range(MLA_H)
        s = [_dot_nt(q_ref[h], k) * ATTN_SCALE for h in hs]
        if diagonal:
            causal = (lax.broadcasted_iota(jnp.int32, (tq, tq), 1)
                      <= lax.broadcasted_iota(jnp.int32, (tq, tq), 0))
            s = [jnp.where(causal, s[h], -jnp.inf) for h in hs]
        m_prev = [m_ref[h] for h in hs]
        m_new = [jnp.maximum(m_prev[h], jnp.max(s[h], -1, keepdims=True)) for h in hs]
        p = [jnp.exp(s[h] - rep(m_new[h], tq)).astype(BF16) for h in hs]
        for h in hs:
            acc_ref[h] = rep(jnp.exp(m_prev[h] - m_new[h]), KW) * acc_ref[h] + _dot(p[h], k)
            m_ref[h] = m_new[h]

    def body(kb, carry):
        update(kb, False)
        return carry

    lax.fori_loop(0, i, body, 0)
    update(i, True)
    y = None
    for h in range(MLA_H):
        acc = acc_ref[h]
        o = (acc[:, :MLA_KV] / acc[:, KW - 1:KW]).astype(BF16)
        yh = _dot(o, wuv_ref[h])
        y = yh if y is None else y + yh
    o_ref[...] = y


def _attn_prompt(q16, kcat, wuv, l, bp, t):
    tq = _pick(t, (512, 256))
    nq = t // tq
    return pl.pallas_call(
        functools.partial(_attn_prompt_kernel, tq=tq),
        out_shape=jax.ShapeDtypeStruct((bp * t, BW), F32),
        grid=(bp, nq),
        in_specs=[pl.BlockSpec((MLA_H, tq, KW), lambda b, i: (0, b * nq + i, 0)),
                  pl.BlockSpec((t, KW), lambda b, i: (b, 0)),
                  _layer_spec((MLA_H, MLA_KV, BW), l)],
        out_specs=pl.BlockSpec((tq, BW), lambda b, i: (b * nq + i, 0)),
        scratch_shapes=[pltpu.VMEM((MLA_H, tq, 128), F32), pltpu.VMEM((MLA_H, tq, KW), F32)],
        compiler_params=_params(("parallel", "parallel")),
        name="mla_prompt",
    )(q16, kcat, wuv)


def _attn_sample_kernel(pt_ref, q_ref, cn_ref, pn_ref, ckv_hbm, kpe_hbm, wuv_ref, o_ref,
                        kbuf, pbuf, sem, *, l, ts, n_pages):
    b = pl.program_id(0)
    slot = b % 2

    last = pl.num_programs(0) - 1
    nxt = jnp.minimum(b + 1, last)

    def page_copies(bb, p, sl):
        pg = pt_ref[bb, p]
        dst = pl.ds(p * PAGE, PAGE) if isinstance(p, int) else pl.ds(pl.multiple_of(p * PAGE, PAGE), PAGE)
        return (pltpu.make_async_copy(ckv_hbm.at[l, pg], kbuf.at[sl, dst, :], sem.at[0, sl]),
                pltpu.make_async_copy(kpe_hbm.at[l, pg], pbuf.at[sl, :, dst], sem.at[1, sl]))

    def fetch_loop(bb, sl, wait):
        def body(p, carry):
            for cp in page_copies(bb, p, sl):
                cp.wait() if wait else cp.start()
            return carry
        lax.fori_loop(0, n_pages, body, 0)

    @pl.when(b == 0)
    def _():
        fetch_loop(0, 0, False)

    fetch_loop(b, slot, True)
    for p in range(n_pages):
        for cp in page_copies(nxt, p, 1 - slot):
            cp.start()

    rows = MLA_H * ts
    q = q_ref[...].reshape(rows, QK)
    ql = q[:, :MLA_KV].astype(BF16)
    qp = q[:, MLA_KV:].astype(BF16)
    kc = kbuf[slot].astype(BF16)
    kp = pbuf[slot].astype(BF16)
    cn = cn_ref[...].astype(BF16)
    pn = pn_ref[...].astype(BF16)
    s = (_dot_nt(ql, kc) + _dot(qp, kp)) * ATTN_SCALE
    sn = (_dot_nt(ql, cn) + _dot_nt(qp, pn)) * ATTN_SCALE
    t = lax.broadcasted_iota(jnp.int32, (rows, ts), 0) % ts
    c = lax.broadcasted_iota(jnp.int32, (rows, ts), 1)
    sn = jnp.where(c <= t, sn, -jnp.inf)
    m = jnp.maximum(jnp.max(s, -1, keepdims=True), jnp.max(sn, -1, keepdims=True))
    p = jnp.exp(s - m)
    pnew = jnp.exp(sn - m)
    den = jnp.sum(p, -1, keepdims=True) + jnp.sum(pnew, -1, keepdims=True)
    o = ((_dot(p.astype(BF16), kc) + _dot(pnew.astype(BF16), cn)) / den).astype(BF16)
    y = _dot(o[0:ts], wuv_ref[0])
    for h in range(1, MLA_H):
        y += _dot(o[h * ts:(h + 1) * ts], wuv_ref[h])
    o_ref[...] = y

    @pl.when(b == last)
    def _():
        fetch_loop(nxt, 1 - slot, True)


def _attn_sample(page_table, qcat, ckv, kpe, cache_ckv, cache_kpe, wuv, l, row0, bs, ts):
    n_pages = page_table.shape[1]
    past = n_pages * PAGE
    blk0 = row0 // ts
    grid_spec = pltpu.PrefetchScalarGridSpec(
        num_scalar_prefetch=1,
        grid=(bs,),
        in_specs=[pl.BlockSpec((MLA_H, ts, QK), lambda b, pt: (0, blk0 + b, 0)),
                  pl.BlockSpec((ts, MLA_KV), lambda b, pt: (blk0 + b, 0)),
                  pl.BlockSpec((ts, MLA_ROPE), lambda b, pt: (blk0 + b, 0)),
                  pl.BlockSpec(memory_space=pl.ANY), pl.BlockSpec(memory_space=pl.ANY),
                  pl.BlockSpec((None, MLA_H, MLA_KV, BW), lambda b, pt: (l, 0, 0, 0))],
        out_specs=pl.BlockSpec((ts, BW), lambda b, pt: (b, 0)),
        scratch_shapes=[pltpu.VMEM((2, past, MLA_KV), F32), pltpu.VMEM((2, MLA_ROPE, past), F32),
                        pltpu.SemaphoreType.DMA((2, 2))])
    return pl.pallas_call(
        functools.partial(_attn_sample_kernel, l=l, ts=ts, n_pages=n_pages),
        out_shape=jax.ShapeDtypeStruct((bs * ts, BW), F32),
        grid_spec=grid_spec,
        compiler_params=_params(("arbitrary",)),
        name="mla_sample",
    )(page_table, qcat, ckv, kpe, cache_ckv, cache_kpe, wuv)


def _split3(x):
    hi = x.astype(BF16)
    r = x - hi.astype(F32)
    mid = r.astype(BF16)
    lo = (r - mid.astype(F32)).astype(BF16)
    return hi, mid, lo


def _dot_01x(m, x):
    w = x.shape[1]
    r = _dot(m, jnp.concatenate(_split3(x), axis=1))
    return r[:, 0:w] + r[:, w:2 * w] + r[:, 2 * w:3 * w]


def _dot_x01(x, m):
    n = x.shape[0]
    r = _dot(jnp.concatenate(_split3(x), axis=0), m)
    return r[0:n] + r[n:2 * n] + r[2 * n:3 * n]


def _iota(shape, dim):
    return lax.broadcasted_iota(jnp.int32, shape, dim)


def _block_diag_mask(rows, row_blk, cols, col_blk):
    return _iota((rows, cols), 0) // row_blk == _iota((rows, cols), 1) // col_blk


def _tile_bd(x16, mask):
    return jnp.where(mask, jnp.concatenate([x16] * RW_H, axis=0), jnp.zeros((), BF16))


def _rwkv_streams(z, prev, pr):
    ones_bd = jnp.where(_block_diag_mask(BW, RW_HD, BW, RW_HD), 1.0, 0.0).astype(BF16)
    zm = z + (prev - z) * pr["mu"][...]
    r = zm[:, 0:BW]
    k = zm[:, BW:2 * BW]
    v = zm[:, 2 * BW:3 * BW]
    lo = zm[:, 3 * BW:]
    xw = -(pr["w0"][...] + _dot(jnp.tanh(lo).astype(BF16), pr["w2"][...]))
    softplus = jnp.maximum(xw, 0.0) + jnp.log(1.0 + jnp.exp(-jnp.abs(xw)))
    logw = -jnp.exp(-softplus - 0.5)
    a_sig = _sigmoid(pr["a0"][...] + _dot(lo.astype(BF16), pr["a2"][...]))
    g = _dot(_sigmoid(lo).astype(BF16), pr["g2"][...])
    kk = k * pr["k_k"][...]
    kk = kk / jnp.maximum(jnp.sqrt(_dot_x01(kk * kk, ones_bd)), 1e-12)
    k = k * (1.0 + (a_sig - 1.0) * pr["k_a"][...])
    bonus = _dot_x01(r * k * pr["r_k"][...], ones_bd) * v
    return r, k, v, logw, -kk, kk * a_sig, g, bonus


def _rwkv_chunk_prep(chunks, c):
    cw = RW_H * c
    n = range(len(chunks))
    bd_rows = _block_diag_mask(cw, c, BW, RW_HD)
    bd_sq = _block_diag_mask(cw, c, cw, c)
    tt = _iota((c, cw), 0)
    ss = _iota((c, cw), 1) % c
    tri = jnp.where(_iota((c, c), 1) <= _iota((c, c), 0), 1.0, 0.0).astype(BF16)
    r, k, v, logw, a, bm = zip(*chunks)
    cl = [_dot_01x(tri, logw[i]) for i in n]
    cl_last = [cl[i][c - 1:c, :] for i in n]
    e_out = [jnp.exp(-cl[i]) for i in n]
    ar = [jnp.concatenate([a[i] * jnp.exp(cl[i] - logw[i]), r[i] * jnp.exp(cl[i])], axis=0).astype(BF16)
          for i in n]
    kb_bd = [jnp.concatenate([_tile_bd((k[i] * e_out[i]).astype(BF16), bd_rows),
                              _tile_bd((bm[i] * e_out[i]).astype(BF16), bd_rows)], axis=0) for i in n]
    gm = [_dot_nt(ar[i], kb_bd[i]) for i in n]
    l_ak = [jnp.where(ss < tt, gm[i][0:c, 0:cw], 0.0) for i in n]
    l_ab = [jnp.where(ss < tt, gm[i][0:c, cw:2 * cw], 0.0) for i in n]
    m_rk = [jnp.where(ss <= tt, gm[i][c:2 * c, 0:cw], 0.0) for i in n]
    m_rb = [jnp.where(ss <= tt, gm[i][c:2 * c, cw:2 * cw], 0.0) for i in n]
    lv = [_dot(jnp.concatenate([l_ak[i], m_rk[i]], axis=0).astype(BF16), _tile_bd(v[i].astype(BF16), bd_rows))
          for i in n]
    p, q = list(l_ab), list(l_ab)
    for _ in range(max(1, (c - 1).bit_length())):
        pq = [_dot(jnp.concatenate([p[i], q[i]], axis=0).astype(BF16), _tile_bd(q[i].astype(BF16), bd_sq))
              for i in n]
        p = [p[i] + pq[i][0:c] for i in n]
        q = [pq[i][c:2 * c] for i in n]
    out = []
    for i in n:
        e_end = jnp.exp(cl_last[i] - cl[i])
        pm = jnp.concatenate([p[i], m_rb[i]], axis=0).astype(BF16)
        kb_end = jnp.concatenate([k[i] * e_end, bm[i] * e_end], axis=0).astype(BF16)
        out.append((ar[i], lv[i], pm, kb_end, jnp.exp(cl_last[i])))
    return out


def _rwkv_chunk_apply(states, preps, v16s, c):
    n = range(len(states))
    bd_rows = _block_diag_mask(RW_H * c, c, BW, RW_HD)
    bd_state = _block_diag_mask(BW, RW_HD, BW, RW_HD)
    ar, lv, pm, kb_end, decay = zip(*preps)
    xs = [_dot_nt(ar[i], states[i].astype(BF16)) for i in n]
    x0 = [xs[i][0:c] + lv[i][0:c] for i in n]
    sa16 = [(x0[i] + _dot(pm[i][0:c], _tile_bd(x0[i].astype(BF16), bd_rows))).astype(BF16) for i in n]
    s_new = [jnp.where(bd_state, states[i] * decay[i]
                       + _dot_tn(jnp.concatenate([v16s[i], sa16[i]], axis=0), kb_end[i]), 0.0) for i in n]
    y = [xs[i][c:2 * c] + lv[i][c:2 * c] + _dot(pm[i][c:2 * c], _tile_bd(sa16[i], bd_rows)) for i in n]
    return y, s_new


def _rwkv_post(y, g, bonus_g, lng, lnb):
    ones_bd = jnp.where(_block_diag_mask(BW, RW_HD, BW, RW_HD), 1.0, 0.0).astype(BF16)
    mean = _dot_x01(y, ones_bd) * (1.0 / RW_HD)
    yc = y - mean
    var = _dot_x01(yc * yc, ones_bd) * (1.0 / RW_HD)
    return (yc * lax.rsqrt(var + GN_EPS) * lng + lnb) * g + bonus_g


_RW_PARAMS = ("mu", "w0", "w2", "a0", "a2", "g2", "k_k", "k_a", "r_k")


def _rw_param_specs(l):
    vec = lambda w: _layer_spec((1, w), l)
    lora = lambda: _layer_spec((128, BW), l)
    return [vec(RW_IN), vec(BW), lora(), vec(BW), lora(), lora(), vec(BW), vec(BW), vec(BW)]


def _rwkv_fused_kernel(z_ref, sh0_ref, s0_ref, *refs, bb, c):
    pr = dict(zip(_RW_PARAMS, refs[:len(_RW_PARAMS)]))
    lng_ref, lnb_ref, y_ref, s_ref, zb_ref = refs[len(_RW_PARAMS):]
    z = z_ref[...]
    for b in range(bb):
        zb_ref[b, 7:8, :] = sh0_ref[b]
        zb_ref[b, 8:8 + c, :] = z[b * c:(b + 1) * c, :]
    prev = jnp.concatenate([zb_ref[b, 7:7 + c, :] for b in range(bb)], axis=0)
    r, k, v, logw, a, bm, g, bonus = _rwkv_streams(z, prev, pr)
    spread = jnp.where(_iota((RW_HD, BW), 1) % RW_HD == _iota((RW_HD, BW), 0), 1.0, 0.0).astype(BF16)
    gather = jnp.where(_iota((BW, RW_HD), 0) % RW_HD == _iota((BW, RW_HD), 1), 1.0, 0.0).astype(BF16)
    bd_state = _block_diag_mask(BW, RW_HD, BW, RW_HD)
    sls = [slice(b * c, (b + 1) * c) for b in range(bb)]
    preps = _rwkv_chunk_prep([(r[sl], k[sl], v[sl], logw[sl], a[sl], bm[sl]) for sl in sls], c)
    states = [jnp.where(bd_state, _dot_x01(s0_ref[b], spread), 0.0) for b in range(bb)]
    ys, s_new = _rwkv_chunk_apply(states, preps, [v[sl].astype(BF16) for sl in sls], c)
    for b in range(bb):
        s_ref[b] = _dot_x01(s_new[b], gather)
    y_ref[...] = _rwkv_post(jnp.concatenate(ys, axis=0), g, bonus * g, lng_ref[...], lnb_ref[...])


def _rwkv_fused(zr, row0, bsz, c, shift0, s0, rp, l, bb):
    blk0 = row0 // (bb * c)
    vec = lambda w: _layer_spec((1, w), l)
    return pl.pallas_call(
        functools.partial(_rwkv_fused_kernel, bb=bb, c=c),
        out_shape=(jax.ShapeDtypeStruct((bsz * c, BW), F32), jax.ShapeDtypeStruct((bsz, BW, RW_HD), F32)),
        grid=(bsz // bb,),
        in_specs=[pl.BlockSpec((bb * c, RW_IN), lambda i: (blk0 + i, 0)),
                  pl.BlockSpec((bb, 1, RW_IN), lambda i: (i, 0, 0)),
                  pl.BlockSpec((bb, BW, RW_HD), lambda i: (i, 0, 0))] + _rw_param_specs(l) + [vec(BW), vec(BW)],
        out_specs=(pl.BlockSpec((bb * c, BW), lambda i: (i, 0)),
                   pl.BlockSpec((bb, BW, RW_HD), lambda i: (i, 0, 0))),
        scratch_shapes=[pltpu.VMEM((bb, 8 + c, RW_IN), F32)],
        compiler_params=_params(("parallel",)),
        name="rwkv7_fused",
    )(zr, shift0, s0, *[rp[n] for n in _RW_PARAMS], rp["ln_g"], rp["ln_b"])


def _rwkv_prep_kernel(*refs, bsz, nc, c):
    z_refs, tail_refs = refs[0:bsz], refs[bsz:2 * bsz]
    sh0_ref = refs[2 * bsz]
    pr = dict(zip(_RW_PARAMS, refs[2 * bsz + 1:2 * bsz + 1 + len(_RW_PARAMS)]))
    ar_ref, lv_ref, pm_ref, kbe_ref, dec_ref, v_ref, g_ref, bg_ref, zb_ref = refs[2 * bsz + 1 + len(_RW_PARAMS):]
    first = pl.program_id(0) == 0
    rows = nc * c
    slabs = []
    for b in range(bsz):
        z = z_refs[b][...]
        zb_ref[7:8, :] = jnp.where(first, sh0_ref[b], tail_refs[b][7:8, :])
        zb_ref[8:8 + rows, :] = z
        r, k, v, logw, a, bm, g, bonus = _rwkv_streams(z, zb_ref[7:7 + rows, :], pr)
        v_ref[b] = v.astype(BF16)
        g_ref[b] = g
        bg_ref[b] = bonus * g
        slabs.append((r, k, v, logw, a, bm))
    chunks = [tuple(x[n * c:(n + 1) * c] for x in slab) for slab in slabs for n in range(nc)]
    for i, (ar, lv, pm, kb_end, decay) in enumerate(_rwkv_chunk_prep(chunks, c)):
        b, n = divmod(i, nc)
        ar_ref[b, n] = ar
        lv_ref[b, n] = lv
        pm_ref[b, n] = pm
        kbe_ref[b, n] = kb_end
        dec_ref[b, n] = decay


def _rwkv_apply_kernel(ar_ref, lv_ref, pm_ref, kbe_ref, dec_ref, v_ref, s0_ref, y_ref, s_ref, *, bsz, c):
    @pl.when(pl.program_id(0) == 0)
    def _():
        s_ref[...] = s0_ref[...]

    preps = [(ar_ref[b, 0], lv_ref[b, 0], pm_ref[b, 0], kbe_ref[b, 0], dec_ref[b, 0]) for b in range(bsz)]
    ys, s_new = _rwkv_chunk_apply([s_ref[b] for b in range(bsz)], preps, [v_ref[b] for b in range(bsz)], c)
    for b in range(bsz):
        s_ref[b] = s_new[b]
        y_ref[b] = ys[b]


def _rwkv_post_kernel(y_ref, g_ref, bg_ref, lng_ref, lnb_ref, o_ref):
    o_ref[...] = _rwkv_post(y_ref[...], g_ref[...], bg_ref[...], lng_ref[...], lnb_ref[...])


def _rwkv_long(zr, bsz, t, shift0, s0, rp, l, c, nc):
    rows = nc * c
    n_ch, n_steps = t // c, t // rows
    vec = lambda w: _layer_spec((1, w), l)
    z_specs = [pl.BlockSpec((rows, RW_IN), lambda j, b=b: (b * n_steps + j, 0)) for b in range(bsz)]
    tail_specs = [pl.BlockSpec((8, RW_IN), lambda j, b=b: (jnp.maximum((b * t + j * rows) // 8 - 1, 0), 0))
                  for b in range(bsz)]
    chunked = lambda h, w, dt: jax.ShapeDtypeStruct((bsz, n_ch, h, w), dt)
    chunk_out = lambda h, w: pl.BlockSpec((bsz, nc, h, w), lambda j: (0, j, 0, 0))
    tok_out = lambda: pl.BlockSpec((bsz, rows, BW), lambda j: (0, j, 0))
    ar, lv, pm, kbe, dec, v16, g, bg = pl.pallas_call(
        functools.partial(_rwkv_prep_kernel, bsz=bsz, nc=nc, c=c),
        out_shape=(chunked(2 * c, BW, BF16), chunked(2 * c, BW, F32), chunked(2 * c, RW_H * c, BF16),
                   chunked(2 * c, BW, BF16), chunked(1, BW, F32),
                   jax.ShapeDtypeStruct((bsz, t, BW), BF16), jax.ShapeDtypeStruct((bsz, t, BW), F32),
                   jax.ShapeDtypeStruct((bsz, t, BW), F32)),
        grid=(n_steps,),
        in_specs=z_specs + tail_specs + [_const_spec((bsz, 1, RW_IN))] + _rw_param_specs(l),
        out_specs=(chunk_out(2 * c, BW), chunk_out(2 * c, BW), chunk_out(2 * c, RW_H * c), chunk_out(2 * c, BW),
                   chunk_out(1, BW), tok_out(), tok_out(), tok_out()),
        scratch_shapes=[pltpu.VMEM((8 + rows, RW_IN), F32)],
        compiler_params=_params(("parallel",)),
        name="rwkv7_prep",
    )(*([zr] * (2 * bsz)), shift0, *[rp[n] for n in _RW_PARAMS])
    chunk_in = lambda h, w: pl.BlockSpec((bsz, 1, h, w), lambda j: (0, j, 0, 0))
    y_raw, s_new = pl.pallas_call(
        functools.partial(_rwkv_apply_kernel, bsz=bsz, c=c),
        out_shape=(jax.ShapeDtypeStruct((bsz, t, BW), F32), jax.ShapeDtypeStruct((bsz, BW, BW), F32)),
        grid=(n_ch,),
        in_specs=[chunk_in(2 * c, BW), chunk_in(2 * c, BW), chunk_in(2 * c, RW_H * c), chunk_in(2 * c, BW),
                  chunk_in(1, BW), pl.BlockSpec((bsz, c, BW), lambda j: (0, j, 0)), _const_spec((bsz, BW, BW))],
        out_specs=(pl.BlockSpec((bsz, c, BW), lambda j: (0, j, 0)), _const_spec((bsz, BW, BW))),
        compiler_params=_params(("arbitrary",)),
        name="rwkv7_apply",
    )(ar, lv, pm, kbe, dec, v16, s0)
    n = bsz * t
    tm = _pick(n, (512, 256, 128, 64))
    row = pl.BlockSpec((tm, BW), lambda i: (i, 0))
    y = pl.pallas_call(
        _rwkv_post_kernel,
        out_shape=jax.ShapeDtypeStruct((n, BW), F32),
        grid=(n // tm,),
        in_specs=[row, row, row, vec(BW), vec(BW)],
        out_specs=row,
        compiler_params=_params(("parallel",)),
        name="rwkv7_post",
    )(y_raw.reshape(n, BW), g.reshape(n, BW), bg.reshape(n, BW), rp["ln_g"], rp["ln_b"])
    return y, s_new


def _conv_kernel(u_ref, buf_ref, w_ref, cb_ref, g_ref, b_ref, y_ref, nb_ref, xp_ref, *, tt):
    j = pl.program_id(1)
    halo = CONV_K - 1
    top = 32 - halo

    @pl.when(j == 0)
    def _():
        xp_ref[:, top:32, :] = buf_ref[...]

    xp_ref[:, 32:32 + tt, :] = u_ref[...].reshape(xp_ref.shape[0], tt, BW)
    acc = xp_ref[:, top:top + tt, :] * w_ref[0:1, :] + cb_ref[...]
    for k in range(1, CONV_K):
        acc = acc + xp_ref[:, top + k:top + k + tt, :] * w_ref[k:k + 1, :]
    yn = _layer_norm(acc, g_ref[...], b_ref[...])
    y_ref[...] = yn * _sigmoid(yn)
    new = xp_ref[:, top + tt:32 + tt, :]
    xp_ref[:, top:32, :] = new
    nb_ref[...] = new


def _conv(u, row0, bsz, t, buf, cp, l, bb, tt):
    assert bb == 1 or tt == t
    halo = CONV_K - 1
    nt = t // tt
    blk0 = row0 // (bb * tt)
    vec = lambda: _layer_spec((1, BW), l)
    return pl.pallas_call(
        functools.partial(_conv_kernel, tt=tt),
        out_shape=(jax.ShapeDtypeStruct((bsz, t, BW), F32), jax.ShapeDtypeStruct((bsz, halo, BW), F32)),
        grid=(bsz // bb, t // tt),
        in_specs=[pl.BlockSpec((bb * tt, BW), lambda i, j: (blk0 + i * nt + j, 0)),
                  pl.BlockSpec((bb, halo, BW), lambda i, j: (i, 0, 0)),
                  _layer_spec((CONV_K, BW), l), vec(), vec(), vec()],
        out_specs=(pl.BlockSpec((bb, tt, BW), lambda i, j: (i, j, 0)),
                   pl.BlockSpec((bb, halo, BW), lambda i, j: (i, 0, 0))),
        scratch_shapes=[pltpu.VMEM((bb, 32 + tt, BW), F32)],
        compiler_params=_params(("parallel", "arbitrary")),
        name="conv_module",
    )(u, buf, cp["w"], cp["b"], cp["ln_g"], cp["ln_b"])


def _gelu_tanh(x):
    return 0.5 * x * (1.0 + jnp.tanh(math.sqrt(2.0 / math.pi) * (x + 0.044715 * (x * x * x))))


def _s5_out(u, hre, him, cre_ref, cim_ref, d_ref, gw_ref, gb_ref):
    y = _dot(hre.astype(BF16), cre_ref[...]) - _dot(him.astype(BF16), cim_ref[...]) + d_ref[...] * u
    y = _gelu_tanh(y)
    return y * _sigmoid(_dot(y.astype(BF16), gw_ref[...]) + gb_ref[...])


def _s5_prompt_kernel(u_ref, bre_ref, bim_ref, pre_ref, pim_ref, cre_ref, cim_ref, d_ref, gw_ref, gb_ref,
                      y_ref, hre_ref, him_ref, xre_ref, xim_ref, *, tt):
    j = pl.program_id(1)

    @pl.when(j == 0)
    def _():
        hre_ref[...] = jnp.zeros_like(hre_ref)
        him_ref[...] = jnp.zeros_like(him_ref)

    u = u_ref[...]
    ub = u.astype(BF16)
    xre = _dot(ub, bre_ref[...])
    xim = _dot(ub, bim_ref[...])
    xre = xre.reshape(tt // 8, 8, S5_N)
    xim = xim.reshape(tt // 8, 8, S5_N)
    for s in range(3):
        d = 1 << s
        are = pre_ref[s]
        aim = pim_ref[s]
        sre = pltpu.roll(xre, d, 1)
        sim = pltpu.roll(xim, d, 1)
        xre, xim = xre + (are * sre - aim * sim), xim + (are * sim + aim * sre)
    xre_ref[...] = xre.reshape(tt, S5_N)
    xim_ref[...] = xim.reshape(tt, S5_N)
    t8re = pre_ref[3]
    t8im = pim_ref[3]

    def group(g, carry):
        cre, cim = carry
        rows = pl.ds(pl.multiple_of(g * 8, 8), 8)
        gre = xre_ref[rows, :] + (t8re * cre - t8im * cim)
        gim = xim_ref[rows, :] + (t8re * cim + t8im * cre)
        xre_ref[rows, :] = gre
        xim_ref[rows, :] = gim
        return (jnp.broadcast_to(gre[7:8, :], (8, S5_N)), jnp.broadcast_to(gim[7:8, :], (8, S5_N)))

    cre, cim = lax.fori_loop(0, tt // 8, group, (jnp.broadcast_to(hre_ref[...], (8, S5_N)),
                                                 jnp.broadcast_to(him_ref[...], (8, S5_N))))
    hre_ref[...] = cre[0:1, :]
    him_ref[...] = cim[0:1, :]
    y_ref[...] = _s5_out(u, xre_ref[...], xim_ref[...], cre_ref, cim_ref, d_ref, gw_ref, gb_ref)


def _s5_prompt(u, bsz, t, sp, l, tt):
    nt = t // tt
    return pl.pallas_call(
        functools.partial(_s5_prompt_kernel, tt=tt),
        out_shape=(jax.ShapeDtypeStruct((bsz, t, BW), F32),
                   jax.ShapeDtypeStruct((bsz, 1, S5_N), F32), jax.ShapeDtypeStruct((bsz, 1, S5_N), F32)),
        grid=(bsz, nt),
        in_specs=[pl.BlockSpec((tt, BW), lambda i, j: (i * nt + j, 0)),
                  _layer_spec((BW, S5_N), l), _layer_spec((BW, S5_N), l),
                  _layer_spec((4, 8, S5_N), l), _layer_spec((4, 8, S5_N), l),
                  _layer_spec((S5_N, BW), l), _layer_spec((S5_N, BW), l),
                  _layer_spec((1, BW), l), _layer_spec((BW, BW), l), _layer_spec((1, BW), l)],
        out_specs=(pl.BlockSpec((None, tt, BW), lambda i, j: (i, j, 0)),
                   pl.BlockSpec((None, 1, S5_N), lambda i, j: (i, 0, 0)),
                   pl.BlockSpec((None, 1, S5_N), lambda i, j: (i, 0, 0))),
        scratch_shapes=[pltpu.VMEM((tt, S5_N), F32), pltpu.VMEM((tt, S5_N), F32)],
        compiler_params=_params(("parallel", "arbitrary")),
        name="s5_prompt",
    )(u, sp["bre"], sp["bim"], sp["pre"], sp["pim"], sp["cre"], sp["cim"], sp["d"], sp["glu_w"], sp["glu_b"])


def _s5_sample_kernel(u_ref, h0re_ref, h0im_ref, bre_ref, bim_ref, pre_ref, pim_ref, cre_ref, cim_ref,
                      d_ref, gw_ref, gb_ref, y_ref, hre_ref, him_ref, *, ts):
    lre = pre_ref[3][0:1, :]
    lim = pim_ref[3][0:1, :]
    hre = h0re_ref[...]
    him = h0im_ref[...]
    for t in range(ts):
        u = u_ref[t]
        ub = u.astype(BF16)
        hre, him = (lre * hre - lim * him + _dot(ub, bre_ref[...]),
                    lre * him + lim * hre + _dot(ub, bim_ref[...]))
        y_ref[t] = _s5_out(u, hre, him, cre_ref, cim_ref, d_ref, gw_ref, gb_ref)
    hre_ref[...] = hre
    him_ref[...] = him


def _s5_sample(u_tm, h0re, h0im, sp, l):
    ts, bsz, _ = u_tm.shape
    return pl.pallas_call(
        functools.partial(_s5_sample_kernel, ts=ts),
        out_shape=(jax.ShapeDtypeStruct((ts, bsz, BW), F32),
                   jax.ShapeDtypeStruct((bsz, S5_N), F32), jax.ShapeDtypeStruct((bsz, S5_N), F32)),
        grid=(1,),
        in_specs=[_const_spec((ts, bsz, BW)), _const_spec((bsz, S5_N)), _const_spec((bsz, S5_N)),
                  _layer_spec((BW, S5_N), l), _layer_spec((BW, S5_N), l),
                  _layer_spec((4, 8, S5_N), l), _layer_spec((4, 8, S5_N), l),
                  _layer_spec((S5_N, BW), l), _layer_spec((S5_N, BW), l),
                  _layer_spec((1, BW), l), _layer_spec((BW, BW), l), _layer_spec((1, BW), l)],
        out_specs=(_const_spec((ts, bsz, BW)), _const_spec((bsz, S5_N)), _const_spec((bsz, S5_N))),
        compiler_params=_params(("arbitrary",)),
        name="s5_sample",
    )(u_tm, h0re, h0im, sp["bre"], sp["bim"], sp["pre"], sp["pim"], sp["cre"], sp["cim"], sp["d"],
      sp["glu_w"], sp["glu_b"])


def _merge_kernel(h_ref, *refs, p_tiles):
    yp_refs, ys_refs = refs[0:N_BRANCH], refs[N_BRANCH:2 * N_BRANCH]
    wg_ref, bg_ref, wbr_ref, wo_ref, g_ref, b_ref, o_ref = refs[2 * N_BRANCH:]
    is_prompt = pl.program_id(0) < p_tiles
    h = h_ref[...]
    hb = h.astype(BF16)
    m = None
    for n in range(N_BRANCH):
        y = jnp.where(is_prompt, yp_refs[n][...], ys_refs[n][...])
        gate = _sigmoid(_dot(hb, wg_ref[:, n * D_MODEL:(n + 1) * D_MODEL]) + bg_ref[n:n + 1, :])
        term = gate * _dot(y.astype(BF16), wbr_ref[n])
        m = term if m is None else m + term
    out = _dot(m.astype(BF16), wo_ref[...])
    o_ref[...] = _layer_norm(ALPHA * h + out, g_ref[...], b_ref[...])


def _merge(h, ys_p, ys_s, mw, l):
    n = h.shape[0]
    n_p, n_s = ys_p[0].shape[0], ys_s[0].shape[0]
    tm = _pick(math.gcd(n_p, n_s), (512, 256, 128, 64, 32, 16, 8))
    p_tiles, s_tiles = n_p // tm, n_s // tm
    row = lambda w: pl.BlockSpec((tm, w), lambda i: (i, 0))
    p_row = pl.BlockSpec((tm, BW), lambda i: (jnp.minimum(i, p_tiles - 1), 0))
    s_row = pl.BlockSpec((tm, BW), lambda i: (jnp.clip(i - p_tiles, 0, s_tiles - 1), 0))
    return pl.pallas_call(
        functools.partial(_merge_kernel, p_tiles=p_tiles),
        out_shape=jax.ShapeDtypeStruct((n, D_MODEL), F32),
        grid=(n // tm,),
        in_specs=[row(D_MODEL)] + [p_row] * N_BRANCH + [s_row] * N_BRANCH + [
                  _layer_spec((D_MODEL, N_BRANCH * D_MODEL), l), _layer_spec((N_BRANCH, D_MODEL), l),
                  _layer_spec((N_BRANCH, BW, D_MODEL), l), _layer_spec((D_MODEL, D_MODEL), l),
                  _layer_spec((1, D_MODEL), l), _layer_spec((1, D_MODEL), l)],
        out_specs=row(D_MODEL),
        compiler_params=_params(("parallel",)),
        name="branch_merge",
    )(h, *ys_p, *ys_s, mw["wg"], mw["bg"], mw["wbr"], mw["wo"], mw["g"], mw["b"])


def _pad_rows(w, lo, total):
    return jnp.pad(w, ((0, 0), (lo, total - lo - w.shape[1]), (0, 0)))


def _s5_params(a_re, a_im, log_dt, b_re, b_im, c_re, c_im):
    dep = a_re.shape[0]
    dt = jnp.exp(log_dt)[..., None]
    mag = jnp.exp(a_re * dt)
    lam_re, lam_im = mag * jnp.cos(a_im * dt), mag * jnp.sin(a_im * dt)
    den = a_re * a_re + a_im * a_im
    co_re = ((lam_re - 1.0) * a_re + lam_im * a_im) / den
    co_im = (lam_im * a_re - (lam_re - 1.0) * a_im) / den
    bb_re = co_re[..., None] * b_re - co_im[..., None] * b_im
    bb_im = co_re[..., None] * b_im + co_im[..., None] * b_re
    eye = jnp.eye(S5_GROUPS, dtype=F32)
    def bmat(x):
        return jnp.einsum("lgpc,gk->lgckp", x, eye).reshape(dep, BW, S5_N)
    def cmat(x):
        return jnp.einsum("lgcp,gk->lgpkc", x, eye).reshape(dep, S5_N, BW)
    lr, li = lam_re.reshape(dep, 1, S5_N), lam_im.reshape(dep, 1, S5_N)
    row = jnp.arange(8)[None, :, None]
    pre, pim = [], []
    xr, xi = lr, li
    for s in range(3):
        pre.append(jnp.where(row >= (1 << s), xr, 0.0))
        pim.append(jnp.where(row >= (1 << s), xi, 0.0))
        xr, xi = xr * xr - xi * xi, 2.0 * xr * xi
    seq_r, seq_i = [lr], [li]
    for _ in range(7):
        xr, xi = seq_r[-1], seq_i[-1]
        seq_r.append(xr * lr - xi * li)
        seq_i.append(xr * li + xi * lr)
    pre.append(jnp.concatenate(seq_r, axis=1))
    pim.append(jnp.concatenate(seq_i, axis=1))
    return dict(bre=bmat(bb_re).astype(BF16), bim=bmat(bb_im).astype(BF16),
                pre=jnp.stack(pre, axis=1), pim=jnp.stack(pim, axis=1),
                cre=cmat(c_re).astype(BF16), cim=cmat(c_im).astype(BF16))


def _rope_tables(pos):
    inv = 1.0 / (ROPE_BASE ** (jnp.arange(0, MLA_ROPE, 2, dtype=F32) / MLA_ROPE))
    ang = pos.astype(F32)[:, None] * inv[None, :]
    cos, sin = jnp.cos(ang), jnp.sin(ang)
    return jnp.concatenate([cos, cos], -1), jnp.concatenate([-sin, sin], -1)


def kernel(x_prompt, x_sample, cache_ckv, cache_kpe, state_rwkv, state_rwkv_shift, state_conv, state_s5_re, state_s5_im, page_table, ffn1_w_in, ffn1_w_down, ln1_g, ln1_b, w_in, b_gate, mla_q_norm, mla_w_uq, mla_kv_norm, mla_w_uk, mla_w_uv, rwkv_mu, rwkv_w0, rwkv_w2, rwkv_a0, rwkv_a2, rwkv_g2, rwkv_k_k, rwkv_k_a, rwkv_r_k, rwkv_ln_g, rwkv_ln_b, conv_w, conv_b, conv_ln_g, conv_ln_b, s5_a_re, s5_a_im, s5_log_dt, s5_b_re, s5_b_im, s5_c_re, s5_c_im, s5_d, s5_glu_w, s5_glu_b, w_branch, w_out, ln2_g, ln2_b, ffn2_w_in, ffn2_w_down, ln3_g, ln3_b):
    bp, tp, _ = x_prompt.shape
    bs, ts, _ = x_sample.shape
    depth = w_in.shape[0]
    n_p, n_s = bp * tp, bs * ts
    past = page_table.shape[1] * PAGE
    halo = CONV_K - 1
    vec = lambda w: w[:, None, :]

    o_kv = MLA_Q
    o_kp = o_kv + MLA_KV
    o_r = o_kp + MLA_ROPE
    o_c = o_r + RW_IN
    o_s = o_c + 2 * BW
    o_g = o_s + BW
    half = MLA_ROPE // 2
    wkp = w_in[:, :, o_kp:o_r]
    uq = mla_w_uq.reshape(depth, MLA_Q, MLA_H, MLA_NOPE + MLA_ROPE)
    uq_pe = uq[..., MLA_NOPE:]
    swap = lambda x: jnp.concatenate([x[..., half:], x[..., :half]], -1)
    to_heads = lambda x: jnp.moveaxis(x, 2, 1).astype(BF16)
    pw = dict(wq=w_in[:, :, :o_kv].astype(BF16), wkv=w_in[:, :, o_kv:o_kp].astype(BF16),
              wkp=wkp.astype(BF16), wkps=swap(wkp).astype(BF16),
              wr=w_in[:, :, o_r:o_c].astype(BF16), wc=w_in[:, :, o_c:o_s].astype(BF16),
              ws=w_in[:, :, o_s:o_g].astype(BF16),
              gq=vec(mla_q_norm), gkv=vec(mla_kv_norm),
              wn=to_heads(uq[..., :MLA_NOPE]), wp=to_heads(uq_pe), wps=to_heads(swap(uq_pe)),
              wuk=jnp.transpose(mla_w_uk, (0, 2, 3, 1)).astype(BF16))
    eye_h = jnp.eye(MLA_H, dtype=F32)
    wuv = jnp.einsum("lrhv,hk->lhrkv", mla_w_uv, eye_h).reshape(depth, MLA_H, MLA_KV, BW).astype(BF16)
    rp = dict(mu=vec(rwkv_mu), w0=vec(rwkv_w0), a0=vec(rwkv_a0),
              w2=_pad_rows(rwkv_w2, 0, 128).astype(BF16), a2=_pad_rows(rwkv_a2, 32, 128).astype(BF16),
              g2=_pad_rows(rwkv_g2, 64, 128).astype(BF16),
              k_k=vec(rwkv_k_k), k_a=vec(rwkv_k_a), r_k=vec(rwkv_r_k.reshape(depth, BW)),
              ln_g=vec(rwkv_ln_g), ln_b=vec(rwkv_ln_b))
    cp = dict(w=conv_w, b=vec(conv_b), ln_g=vec(conv_ln_g), ln_b=vec(conv_ln_b))
    tt_s5 = _pick(tp, (512, 256, 128))
    sp = _s5_params(s5_a_re, s5_a_im, s5_log_dt, s5_b_re, s5_b_im, s5_c_re, s5_c_im)
    sp.update(d=vec(s5_d), glu_w=s5_glu_w.astype(BF16), glu_b=vec(s5_glu_b))
    mw = dict(wg=w_in[:, :, o_g:].astype(BF16), bg=b_gate, wbr=w_branch.astype(BF16), wo=w_out.astype(BF16),
              g=vec(ln2_g), b=vec(ln2_b))
    f1 = (ffn1_w_in.astype(BF16), ffn1_w_down.astype(BF16), vec(ln1_g), vec(ln1_b))
    f2 = (ffn2_w_in.astype(BF16), ffn2_w_down.astype(BF16), vec(ln3_g), vec(ln3_b))

    cos_p, sin_p = _rope_tables(jnp.arange(tp))
    cos_s, sin_s = _rope_tables(past + jnp.arange(ts))
    cos2 = jnp.concatenate([jnp.tile(cos_p, (bp, 1)), jnp.tile(cos_s, (bs, 1))], 0)
    sin2 = jnp.concatenate([jnp.tile(sin_p, (bp, 1)), jnp.tile(sin_s, (bs, 1))], 0)

    c_rw = _pick(tp, (64, 32, 16, 8))
    nc_rw = _pick(tp // c_rw, (4, 2, 1))
    bb_rw = _pick(bs, (8, 4, 2, 1))
    tt_cv = _pick(tp, (512, 256, 128, 64, 32))
    bb_cv = _pick(bs, (32, 16, 8, 4, 2, 1))

    cache_kpe_t = jnp.swapaxes(cache_kpe, 2, 3)
    x = jnp.concatenate([x_prompt.reshape(n_p, D_MODEL), x_sample.reshape(n_s, D_MODEL)], 0)
    outs = [[] for _ in range(14)]
    for l in range(depth):
        h = _ffn_ln(x, *f1, l)
        qcat, q16, ckv, kpe, kcat, zr, u, zs = _prep(h, cos2, sin2, pw, l)
        ya_p = _attn_prompt(q16, kcat, wuv, l, bp, tp)
        ya_s = _attn_sample(page_table, qcat, ckv, kpe, cache_ckv, cache_kpe_t, wuv, l, n_p, bs, ts)
        yb_p, srw_p = _rwkv_long(zr, bp, tp, jnp.zeros((bp, 1, RW_IN), F32), jnp.zeros((bp, BW, BW), F32),
                                 rp, l, c_rw, nc_rw)
        yb_s, srw_s = _rwkv_fused(zr, n_p, bs, ts, state_rwkv_shift[l][:, None, :],
                                  state_rwkv[l].reshape(bs, BW, RW_HD), rp, l, bb_rw)
        yc_p, cv_p = _conv(u, 0, bp, tp, jnp.zeros((bp, halo, BW), F32), cp, l, 1, tt_cv)
        yc_s, cv_s = _conv(u, n_p, bs, ts, state_conv[l], cp, l, bb_cv, ts)
        yd_p, s5re_p, s5im_p = _s5_prompt(zs, bp, tp, sp, l, tt_s5)
        yd_s, s5re_s, s5im_s = _s5_sample(jnp.swapaxes(zs[n_p:].reshape(bs, ts, BW), 0, 1),
                                          state_s5_re[l].reshape(bs, S5_N), state_s5_im[l].reshape(bs, S5_N), sp, l)
        yd_s = jnp.swapaxes(yd_s, 0, 1)
        flat = lambda a: a.reshape(-1, BW)
        x = _merge(h, (ya_p, flat(yb_p), flat(yc_p), flat(yd_p)), (ya_s, flat(yb_s), flat(yc_s), flat(yd_s)),
                   mw, l)
        x = _ffn_ln(x, *f2, l)
        shift_p = zr[tp - 1:n_p:tp]
        shift_s = zr[n_p + ts - 1::ts]

        diag = lambda s: jnp.stack([s[:, i * RW_HD:(i + 1) * RW_HD, i * RW_HD:(i + 1) * RW_HD]
                                    for i in range(RW_H)], 1)
        new_p = (ckv[:n_p].reshape(bp, tp, MLA_KV), kpe[:n_p].reshape(bp, tp, MLA_ROPE), diag(srw_p),
                 shift_p, cv_p,s5re_p.reshape(bp, S5_GROUPS, S5_P), s5im_p.reshape(bp, S5_GROUPS, S5_P))
        new_s = (ckv[n_p:].reshape(bs, ts, MLA_KV), kpe[n_p:].reshape(bs, ts, MLA_ROPE),
                 srw_s.reshape(bs, RW_H, RW_HD, RW_HD),
                 shift_s, cv_s,s5re_s.reshape(bs, S5_GROUPS, S5_P), s5im_s.reshape(bs, S5_GROUPS, S5_P))
        for i, a in enumerate(new_p + new_s):
            outs[i].append(a)
    return (x[:n_p].reshape(bp, tp, D_MODEL), x[n_p:].reshape(bs, ts, D_MODEL)) + tuple(jnp.stack(o) for o in outs)
```

```python
import functools
import math

import jax
import jax.numpy as jnp
from jax import lax
from jax.experimental import pallas as pl
from jax.experimental.pallas import tpu as pltpu

F32 = jnp.float32
BF16 = jnp.bfloat16

D_MODEL = 1024
PAGE = 128
BW = D_MODEL // 4
N_BRANCH = 4
MLA_V = 64
MLA_H = BW // MLA_V
MLA_NOPE = 64
MLA_ROPE = 32
MLA_Q = D_MODEL // 4
MLA_KV = D_MODEL // 8
QK = MLA_KV + MLA_ROPE
KW = 256
ROPE_BASE = 10000.0
RW_HD = 64
RW_H = BW // RW_HD
RW_IN = 3 * BW + 128
CONV_K = 31
S5_G = 16
S5_GROUPS = BW // S5_G
S5_P = 64
S5_N = S5_GROUPS * S5_P
D_FF = 2816
FF_CHUNK = 256
DEPTH = 4
ALPHA = (2 * DEPTH) ** 0.25
ATTN_SCALE = (MLA_NOPE + MLA_ROPE) ** -0.5
LN_EPS = 1e-5
RMS_EPS = 1e-6
GN_EPS = 64e-5
VMEM_LIMIT = 56 * 1024 * 1024


def _pick(n, cands):
    for c in cands:
        if n % c == 0:
            return c
    raise ValueError(f"no tile in {cands} divides {n}")


def _const_spec(shape):
    nd = len(shape)
    return pl.BlockSpec(shape, lambda *_: (0,) * nd)


def _layer_spec(shape, l):
    nd = len(shape)
    return pl.BlockSpec((None,) + tuple(shape), lambda *_: (l,) + (0,) * nd)


def _params(sem, vmem=VMEM_LIMIT):
    return pltpu.CompilerParams(dimension_semantics=sem, vmem_limit_bytes=vmem)


def _dot(a, b, **kw):
    return jnp.dot(a, b, preferred_element_type=F32, **kw)


def _dot_nt(a, b, **kw):
    return lax.dot_general(a, b, (((1,), (1,)), ((), ())), preferred_element_type=F32, **kw)


def _dot_tn(a, b, **kw):
    return lax.dot_general(a, b, (((0,), (0,)), ((), ())), preferred_element_type=F32, **kw)


def _layer_norm(x, g, b):
    mu = jnp.mean(x, -1, keepdims=True)
    xc = x - mu
    var = jnp.mean(xc * xc, -1, keepdims=True)
    return xc * lax.rsqrt(var + LN_EPS) * g + b


def _rms_norm(x, g):
    return x * lax.rsqrt(jnp.mean(x * x, -1, keepdims=True) + RMS_EPS) * g


def _sigmoid(x):
    return 1.0 / (1.0 + jnp.exp(-x))


def _ffn_ln_kernel(x_ref, win_ref, wdn_ref, g_ref, b_ref, o_ref, acc_ref):
    x = x_ref[...]
    xb = x.astype(BF16)
    for c in range(D_FF // FF_CHUNK):
        lo = c * FF_CHUNK
        a = _dot(xb, win_ref[:, lo:lo + FF_CHUNK])
        b = _dot(xb, win_ref[:, D_FF + lo:D_FF + lo + FF_CHUNK])
        h = (a * _sigmoid(a) * b).astype(BF16)
        d = _dot(h, wdn_ref[lo:lo + FF_CHUNK, :])
        if c == 0:
            acc_ref[...] = d
        else:
            acc_ref[...] += d
    o_ref[...] = _layer_norm(ALPHA * x + 0.5 * acc_ref[...], g_ref[...], b_ref[...])


def _ffn_ln(x, w_in, w_down, g, b, l):
    n = x.shape[0]
    tm = _pick(n, (512, 256, 128, 64, 32, 16, 8))
    return pl.pallas_call(
        _ffn_ln_kernel,
        out_shape=jax.ShapeDtypeStruct((n, D_MODEL), F32),
        grid=(n // tm,),
        in_specs=[pl.BlockSpec((tm, D_MODEL), lambda i: (i, 0)),
                  _layer_spec((D_MODEL, 2 * D_FF), l), _layer_spec((D_FF, D_MODEL), l),
                  _layer_spec((1, D_MODEL), l), _layer_spec((1, D_MODEL), l)],
        out_specs=pl.BlockSpec((tm, D_MODEL), lambda i: (i, 0)),
        scratch_shapes=[pltpu.VMEM((tm, D_MODEL), F32)],
        compiler_params=_params(("parallel",)),
        name="ffn_ln",
    )(x, w_in, w_down, g, b)


def _prep_kernel(h_ref, cos_ref, sin_ref, wq_ref, wkv_ref, wkp_ref, wkps_ref, wr_ref, wc_ref, ws_ref,
                 gq_ref, gkv_ref, wn_ref, wp_ref, wps_ref, wuk_ref,
                 qcat_ref, q16_ref, ckv_ref, kpe_ref, kcat_ref, zr_ref, u_ref, zs_ref):
    hb = h_ref[...].astype(BF16)
    cos2 = cos_ref[...]
    sin2 = sin_ref[...]
    tm = hb.shape[0]
    zq = _rms_norm(_dot(hb, wq_ref[...]), gq_ref[...]).astype(BF16)
    hs = range(MLA_H)
    qn = [_dot(zq, wn_ref[h]).astype(BF16) for h in hs]
    qp = [_dot(zq, wp_ref[h]) for h in hs]
    qps = [_dot(zq, wps_ref[h]) for h in hs]
    ql = [_dot(qn[h], wuk_ref[h]) for h in hs]
    for h in hs:
        qpe = qp[h] * cos2 + qps[h] * sin2
        qcat_ref[h, :, 0:MLA_KV] = ql[h]
        qcat_ref[h, :, MLA_KV:QK] = qpe
        q16_ref[h, :, 0:MLA_KV] = ql[h].astype(BF16)
        q16_ref[h, :, MLA_KV:QK] = qpe.astype(BF16)
        q16_ref[h, :, QK:KW] = jnp.zeros((tm, KW - QK), BF16)
    ckv = _rms_norm(_dot(hb, wkv_ref[...]), gkv_ref[...])
    kpe = _dot(hb, wkp_ref[...]) * cos2 + _dot(hb, wkps_ref[...]) * sin2
    ckv_ref[...] = ckv
    kpe_ref[...] = kpe
    kcat_ref[:, 0:MLA_KV] = ckv.astype(BF16)
    kcat_ref[:, MLA_KV:QK] = kpe.astype(BF16)
    kcat_ref[:, QK:KW] = jnp.ones((tm, KW - QK), BF16)
    zr_ref[...] = _dot(hb, wr_ref[...])
    zc = _dot(hb, wc_ref[...])
    u_ref[...] = zc[:, :BW] * _sigmoid(zc[:, BW:])
    zs_ref[...] = _dot(hb, ws_ref[...])


def _prep(h, cos2, sin2, pw, l):
    n = h.shape[0]
    tm = _pick(n, (512, 256, 128, 64, 32, 16))
    row = lambda w: pl.BlockSpec((tm, w), lambda i: (i, 0))
    return pl.pallas_call(
        _prep_kernel,
        out_shape=(jax.ShapeDtypeStruct((MLA_H, n, QK), F32),
                   jax.ShapeDtypeStruct((MLA_H, n, KW), BF16),
                   jax.ShapeDtypeStruct((n, MLA_KV), F32),
                   jax.ShapeDtypeStruct((n, MLA_ROPE), F32),
                   jax.ShapeDtypeStruct((n, KW), BF16),
                   jax.ShapeDtypeStruct((n, RW_IN), F32),
                   jax.ShapeDtypeStruct((n, BW), F32),
                   jax.ShapeDtypeStruct((n, BW), F32)),
        grid=(n // tm,),
        in_specs=[row(D_MODEL), row(MLA_ROPE), row(MLA_ROPE),
                  _layer_spec((D_MODEL, MLA_Q), l), _layer_spec((D_MODEL, MLA_KV), l),
                  _layer_spec((D_MODEL, MLA_ROPE), l), _layer_spec((D_MODEL, MLA_ROPE), l),
                  _layer_spec((D_MODEL, RW_IN), l), _layer_spec((D_MODEL, 2 * BW), l),
                  _layer_spec((D_MODEL, BW), l),
                  _layer_spec((1, MLA_Q), l), _layer_spec((1, MLA_KV), l),
                  _layer_spec((MLA_H, MLA_Q, MLA_NOPE), l), _layer_spec((MLA_H, MLA_Q, MLA_ROPE), l),
                  _layer_spec((MLA_H, MLA_Q, MLA_ROPE), l), _layer_spec((MLA_H, MLA_NOPE, MLA_KV), l)],
        out_specs=(pl.BlockSpec((MLA_H, tm, QK), lambda i: (0, i, 0)),
                   pl.BlockSpec((MLA_H, tm, KW), lambda i: (0, i, 0)),
                   row(MLA_KV), row(MLA_ROPE), row(KW), row(RW_IN), row(BW), row(BW)),
        compiler_params=_params(("parallel",)),
        name="mixer_prep",
    )(h, cos2, sin2, pw["wq"], pw["wkv"], pw["wkp"], pw["wkps"], pw["wr"], pw["wc"], pw["ws"],
      pw["gq"], pw["gkv"], pw["wn"], pw["wp"], pw["wps"], pw["wuk"])


def _attn_prompt_kernel(q_ref, k_ref, wuv_ref, o_ref, m_ref, acc_ref, *, tq):
    i = pl.program_id(1)
    rep = lambda x, w: jnp.concatenate([x] * (w // 128), axis=1)
    m_ref[...] = jnp.full(m_ref.shape, -jnp.inf, F32)
    acc_ref[...] = jnp.zeros(acc_ref.shape, F32)

    def update(kb, diagonal):
        k = k_ref[pl.ds(pl.multiple_of(kb * tq, tq), tq), :]
        hs = range(MLA_H)
        s = [_dot_nt(q_ref[h], k) * ATTN_SCALE for h in hs]
        if diagonal:
            causal = (lax.broadcasted_iota(jnp.int32, (tq, tq), 1)
                      <= lax.broadcasted_iota(jnp.int32, (tq, tq), 0))
            s = [jnp.where(causal, s[h], -jnp.inf) for h in hs]
        m_prev = [m_ref[h] for h in hs]
        m_new = [jnp.maximum(m_prev[h], jnp.max(s[h], -1, keepdims=True)) for h in hs]
        p = [jnp.exp(s[h] - rep(m_new[h], tq)).astype(BF16) for h in hs]
        for h in hs:
            acc_ref[h] = rep(jnp.exp(m_prev[h] - m_new[h]), KW) * acc_ref[h] + _dot(p[h], k)
            m_ref[h] = m_new[h]

    def body(kb, carry):
        update(kb, False)
        return carry

    lax.fori_loop(0, i, body, 0)
    update(i, True)
    y = None
    for h in range(MLA_H):
        acc = acc_ref[h]
        o = (acc[:, :MLA_KV] / acc[:, KW - 1:KW]).astype(BF16)
        yh = _dot(o, wuv_ref[h])
        y = yh if y is None else y + yh
    o_ref[...] = y


def _attn_prompt(q16, kcat, wuv, l, bp, t):
    tq = _pick(t, (512, 256))
    nq = t // tq
    return pl.pallas_call(
        functools.partial(_attn_prompt_kernel, tq=tq),
        out_shape=jax.ShapeDtypeStruct((bp * t, BW), F32),
        grid=(bp, nq),
        in_specs=[pl.BlockSpec((MLA_H, tq, KW), lambda b, i: (0, b * nq + i, 0)),
                  pl.BlockSpec((t, KW), lambda b, i: (b, 0)),
                  _layer_spec((MLA_H, MLA_KV, BW), l)],
        out_specs=pl.BlockSpec((tq, BW), lambda b, i: (b * nq + i, 0)),
        scratch_shapes=[pltpu.VMEM((MLA_H, tq, 128), F32), pltpu.VMEM((MLA_H, tq, KW), F32)],
        compiler_params=_params(("parallel", "parallel")),
        name="mla_prompt",
    )(q16, kcat, wuv)


def _attn_sample_kernel(pt_ref, q_ref, cn_ref, pn_ref, ckv_hbm, kpe_hbm, wuv_ref, o_ref,
                        kbuf, pbuf, sem, *, l, ts, n_pages):
    b = pl.program_id(0)
    slot = b % 2

    last = pl.num_programs(0) - 1
    nxt = jnp.minimum(b + 1, last)

    def page_copies(bb, p, sl):
        pg = pt_ref[bb, p]
        dst = pl.ds(p * PAGE, PAGE) if isinstance(p, int) else pl.ds(pl.multiple_of(p * PAGE, PAGE), PAGE)
        return (pltpu.make_async_copy(ckv_hbm.at[l, pg], kbuf.at[sl, dst, :], sem.at[0, sl]),
                pltpu.make_async_copy(kpe_hbm.at[l, pg], pbuf.at[sl, :, dst], sem.at[1, sl]))

    def fetch_loop(bb, sl, wait):
        def body(p, carry):
            for cp in page_copies(bb, p, sl):
                cp.wait() if wait else cp.start()
            return carry
        lax.fori_loop(0, n_pages, body, 0)

    @pl.when(b == 0)
    def _():
        fetch_loop(0, 0, False)

    fetch_loop(b, slot, True)
    for p in range(n_pages):
        for cp in page_copies(nxt, p, 1 - slot):
            cp.start()

    rows = MLA_H * ts
    q = q_ref[...].reshape(rows, QK)
    ql = q[:, :MLA_KV].astype(BF16)
    qp = q[:, MLA_KV:].astype(BF16)
    kc = kbuf[slot].astype(BF16)
    kp = pbuf[slot].astype(BF16)
    cn = cn_ref[...].astype(BF16)
    pn = pn_ref[...].astype(BF16)
    s = (_dot_nt(ql, kc) + _dot(qp, kp)) * ATTN_SCALE
    sn = (_dot_nt(ql, cn) + _dot_nt(qp, pn)) * ATTN_SCALE
    t = lax.broadcasted_iota(jnp.int32, (rows, ts), 0) % ts
    c = lax.broadcasted_iota(jnp.int32, (rows, ts), 1)
    sn = jnp.where(c <= t, sn, -jnp.inf)
    m = jnp.maximum(jnp.max(s, -1, keepdims=True), jnp.max(sn, -1, keepdims=True))
    p = jnp.exp(s - m)
    pnew = jnp.exp(sn - m)
    den = jnp.sum(p, -1, keepdims=True) + jnp.sum(pnew, -1, keepdims=True)
    o = ((_dot(p.astype(BF16), kc) + _dot(pnew.astype(BF16), cn)) / den).astype(BF16)
    y = _dot(o[0:ts], wuv_ref[0])
    for h in range(1, MLA_H):
        y += _dot(o[h * ts:(h + 1) * ts], wuv_ref[h])
    o_ref[...] = y

    @pl.when(b == last)
    def _():
        fetch_loop(nxt, 1 - slot, True)


def _attn_sample(page_table, qcat, ckv, kpe, cache_ckv, cache_kpe, wuv, l, row0, bs, ts):
    n_pages = page_table.shape[1]
    past = n_pages * PAGE
    blk0 = row0 // ts
    grid_spec = pltpu.PrefetchScalarGridSpec(
        num_scalar_prefetch=1,
        grid=(bs,),
        in_specs=[pl.BlockSpec((MLA_H, ts, QK), lambda b, pt: (0, blk0 + b, 0)),
                  pl.BlockSpec((ts, MLA_KV), lambda b, pt: (blk0 + b, 0)),
                  pl.BlockSpec((ts, MLA_ROPE), lambda b, pt: (blk0 + b, 0)),
                  pl.BlockSpec(memory_space=pl.ANY), pl.BlockSpec(memory_space=pl.ANY),
                  pl.BlockSpec((None, MLA_H, MLA_KV, BW), lambda b, pt: (l, 0, 0, 0))],
        out_specs=pl.BlockSpec((ts, BW), lambda b, pt: (b, 0)),
        scratch_shapes=[pltpu.VMEM((2, past, MLA_KV), F32), pltpu.VMEM((2, MLA_ROPE, past), F32),
                        pltpu.SemaphoreType.DMA((2, 2))])
    return pl.pallas_call(
        functools.partial(_attn_sample_kernel, l=l, ts=ts, n_pages=n_pages),
        out_shape=jax.ShapeDtypeStruct((bs * ts, BW), F32),
        grid_spec=grid_spec,
        compiler_params=_params(("arbitrary",)),
        name="mla_sample",
    )(page_table, qcat, ckv, kpe, cache_ckv, cache_kpe, wuv)


def _split3(x):
    hi = x.astype(BF16)
    r = x - hi.astype(F32)
    mid = r.astype(BF16)
    lo = (r - mid.astype(F32)).astype(BF16)
    return hi, mid, lo


def _dot_01x(m, x):
    w = x.shape[1]
    r = _dot(m, jnp.concatenate(_split3(x), axis=1))
    return r[:, 0:w] + r[:, w:2 * w] + r[:, 2 * w:3 * w]


def _dot_x01(x, m):
    n = x.shape[0]
    r = _dot(jnp.concatenate(_split3(x), axis=0), m)
    return r[0:n] + r[n:2 * n] + r[2 * n:3 * n]


def _iota(shape, dim):
    return lax.broadcasted_iota(jnp.int32, shape, dim)


def _block_diag_mask(rows, row_blk, cols, col_blk):
    return _iota((rows, cols), 0) // row_blk == _iota((rows, cols), 1) // col_blk


def _tile_bd(x16, mask):
    return jnp.where(mask, jnp.concatenate([x16] * RW_H, axis=0), jnp.zeros((), BF16))


def _rwkv_streams(z, prev, pr):
    ones_bd = jnp.where(_block_diag_mask(BW, RW_HD, BW, RW_HD), 1.0, 0.0).astype(BF16)
    zm = z + (prev - z) * pr["mu"][...]
    r = zm[:, 0:BW]
    k = zm[:, BW:2 * BW]
    v = zm[:, 2 * BW:3 * BW]
    lo = zm[:, 3 * BW:]
    xw = -(pr["w0"][...] + _dot(jnp.tanh(lo).astype(BF16), pr["w2"][...]))
    softplus = jnp.maximum(xw, 0.0) + jnp.log(1.0 + jnp.exp(-jnp.abs(xw)))
    logw = -jnp.exp(-softplus - 0.5)
    a_sig = _sigmoid(pr["a0"][...] + _dot(lo.astype(BF16), pr["a2"][...]))
    g = _dot(_sigmoid(lo).astype(BF16), pr["g2"][...])
    kk = k * pr["k_k"][...]
    kk = kk / jnp.maximum(jnp.sqrt(_dot_x01(kk * kk, ones_bd)), 1e-12)
    k = k * (1.0 + (a_sig - 1.0) * pr["k_a"][...])
    bonus = _dot_x01(r * k * pr["r_k"][...], ones_bd) * v
    return r, k, v, logw, -kk, kk * a_sig, g, bonus


def _rwkv_chunk_prep(chunks, c):
    cw = RW_H * c
    n = range(len(chunks))
    bd_rows = _block_diag_mask(cw, c, BW, RW_HD)
    bd_sq = _block_diag_mask(cw, c, cw, c)
    tt = _iota((c, cw), 0)
    ss = _iota((c, cw), 1) % c
    tri = jnp.where(_iota((c, c), 1) <= _iota((c, c), 0), 1.0, 0.0).astype(BF16)
    r, k, v, logw, a, bm = zip(*chunks)
    cl = [_dot_01x(tri, logw[i]) for i in n]
    cl_last = [cl[i][c - 1:c, :] for i in n]
    e_out = [jnp.exp(-cl[i]) for i in n]
    ar = [jnp.concatenate([a[i] * jnp.exp(cl[i] - logw[i]), r[i] * jnp.exp(cl[i])], axis=0).astype(BF16)
          for i in n]
    kb_bd = [jnp.concatenate([_tile_bd((k[i] * e_out[i]).astype(BF16), bd_rows),
                              _tile_bd((bm[i] * e_out[i]).astype(BF16), bd_rows)], axis=0) for i in n]
    gm = [_dot_nt(ar[i], kb_bd[i]) for i in n]
    l_ak = [jnp.where(ss < tt, gm[i][0:c, 0:cw], 0.0) for i in n]
    l_ab = [jnp.where(ss < tt, gm[i][0:c, cw:2 * cw], 0.0) for i in n]
    m_rk = [jnp.where(ss <= tt, gm[i][c:2 * c, 0:cw], 0.0) for i in n]
    m_rb = [jnp.where(ss <= tt, gm[i][c:2 * c, cw:2 * cw], 0.0) for i in n]
    lv = [_dot(jnp.concatenate([l_ak[i], m_rk[i]], axis=0).astype(BF16), _tile_bd(v[i].astype(BF16), bd_rows))
          for i in n]
    p, q = list(l_ab), list(l_ab)
    for _ in range(max(1, (c - 1).bit_length())):
        pq = [_dot(jnp.concatenate([p[i], q[i]], axis=0).astype(BF16), _tile_bd(q[i].astype(BF16), bd_sq))
              for i in n]
        p = [p[i] + pq[i][0:c] for i in n]
        q = [pq[i][c:2 * c] for i in n]
    out = []
    for i in n:
        e_end = jnp.exp(cl_last[i] - cl[i])
        pm = jnp.concatenate([p[i], m_rb[i]], axis=0).astype(BF16)
        kb_end = jnp.concatenate([k[i] * e_end, bm[i] * e_end], axis=0).astype(BF16)
        out.append((ar[i], lv[i], pm, kb_end, jnp.exp(cl_last[i])))
    return out


def _rwkv_chunk_apply(states, preps, v16s, c):
    n = range(len(states))
    bd_rows = _block_diag_mask(RW_H * c, c, BW, RW_HD)
    bd_state = _block_diag_mask(BW, RW_HD, BW, RW_HD)
    ar, lv, pm, kb_end, decay = zip(*preps)
    xs = [_dot_nt(ar[i], states[i].astype(BF16)) for i in n]
    x0 = [xs[i][0:c] + lv[i][0:c] for i in n]
    sa16 = [(x0[i] + _dot(pm[i][0:c], _tile_bd(x0[i].astype(BF16), bd_rows))).astype(BF16) for i in n]
    s_new = [jnp.where(bd_state, states[i] * decay[i]
                       + _dot_tn(jnp.concatenate([v16s[i], sa16[i]], axis=0), kb_end[i]), 0.0) for i in n]
    y = [xs[i][c:2 * c] + lv[i][c:2 * c] + _dot(pm[i][c:2 * c], _tile_bd(sa16[i], bd_rows)) for i in n]
    return y, s_new


def _rwkv_post(y, g, bonus_g, lng, lnb):
    ones_bd = jnp.where(_block_diag_mask(BW, RW_HD, BW, RW_HD), 1.0, 0.0).astype(BF16)
    mean = _dot_x01(y, ones_bd) * (1.0 / RW_HD)
    yc = y - mean
    var = _dot_x01(yc * yc, ones_bd) * (1.0 / RW_HD)
    return (yc * lax.rsqrt(var + GN_EPS) * lng + lnb) * g + bonus_g


_RW_PARAMS = ("mu", "w0", "w2", "a0", "a2", "g2", "k_k", "k_a", "r_k")


def _rw_param_specs(l):
    vec = lambda w: _layer_spec((1, w), l)
    lora = lambda: _layer_spec((128, BW), l)
    return [vec(RW_IN), vec(BW), lora(), vec(BW), lora(), lora(), vec(BW), vec(BW), vec(BW)]


def _rwkv_fused_kernel(z_ref, sh0_ref, s0_ref, *refs, bb, c):
    pr = dict(zip(_RW_PARAMS, refs[:len(_RW_PARAMS)]))
    lng_ref, lnb_ref, y_ref, s_ref, zb_ref = refs[len(_RW_PARAMS):]
    z = z_ref[...]
    for b in range(bb):
        zb_ref[b, 7:8, :] = sh0_ref[b]
        zb_ref[b, 8:8 + c, :] = z[b * c:(b + 1) * c, :]
    prev = jnp.concatenate([zb_ref[b, 7:7 + c, :] for b in range(bb)], axis=0)
    r, k, v, logw, a, bm, g, bonus = _rwkv_streams(z, prev, pr)
    spread = jnp.where(_iota((RW_HD, BW), 1) % RW_HD == _iota((RW_HD, BW), 0), 1.0, 0.0).astype(BF16)
    gather = jnp.where(_iota((BW, RW_HD), 0) % RW_HD == _iota((BW, RW_HD), 1), 1.0, 0.0).astype(BF16)
    bd_state = _block_diag_mask(BW, RW_HD, BW, RW_HD)
    sls = [slice(b * c, (b + 1) * c) for b in range(bb)]
    preps = _rwkv_chunk_prep([(r[sl], k[sl], v[sl], logw[sl], a[sl], bm[sl]) for sl in sls], c)
    states = [jnp.where(bd_state, _dot_x01(s0_ref[b], spread), 0.0) for b in range(bb)]
    ys, s_new = _rwkv_chunk_apply(states, preps, [v[sl].astype(BF16) for sl in sls], c)
    for b in range(bb):
        s_ref[b] = _dot_x01(s_new[b], gather)
    y_ref[...] = _rwkv_post(jnp.concatenate(ys, axis=0), g, bonus * g, lng_ref[...], lnb_ref[...])


def _rwkv_fused(zr, row0, bsz, c, shift0, s0, rp, l, bb):
    blk0 = row0 // (bb * c)
    vec = lambda w: _layer_spec((1, w), l)
    return pl.pallas_call(
        functools.partial(_rwkv_fused_kernel, bb=bb, c=c),
        out_shape=(jax.ShapeDtypeStruct((bsz * c, BW), F32), jax.ShapeDtypeStruct((bsz, BW, RW_HD), F32)),
        grid=(bsz // bb,),
        in_specs=[pl.BlockSpec((bb * c, RW_IN), lambda i: (blk0 + i, 0)),
                  pl.BlockSpec((bb, 1, RW_IN), lambda i: (i, 0, 0)),
                  pl.BlockSpec((bb, BW, RW_HD), lambda i: (i, 0, 0))] + _rw_param_specs(l) + [vec(BW), vec(BW)],
        out_specs=(pl.BlockSpec((bb * c, BW), lambda i: (i, 0)),
                   pl.BlockSpec((bb, BW, RW_HD), lambda i: (i, 0, 0))),
        scratch_shapes=[pltpu.VMEM((bb, 8 + c, RW_IN), F32)],
        compiler_params=_params(("parallel",)),
        name="rwkv7_fused",
    )(zr, shift0, s0, *[rp[n] for n in _RW_PARAMS], rp["ln_g"], rp["ln_b"])


def _rwkv_prep_kernel(*refs, bsz, nc, c):
    z_refs, tail_refs = refs[0:bsz], refs[bsz:2 * bsz]
    sh0_ref = refs[2 * bsz]
    pr = dict(zip(_RW_PARAMS, refs[2 * bsz + 1:2 * bsz + 1 + len(_RW_PARAMS)]))
    ar_ref, lv_ref, pm_ref, kbe_ref, dec_ref, v_ref, g_ref, bg_ref, zb_ref = refs[2 * bsz + 1 + len(_RW_PARAMS):]
    first = pl.program_id(0) == 0
    rows = nc * c
    slabs = []
    for b in range(bsz):
        z = z_refs[b][...]
        zb_ref[7:8, :] = jnp.where(first, sh0_ref[b], tail_refs[b][7:8, :])
        zb_ref[8:8 + rows, :] = z
        r, k, v, logw, a, bm, g, bonus = _rwkv_streams(z, zb_ref[7:7 + rows, :], pr)
        v_ref[b] = v.astype(BF16)
        g_ref[b] = g
        bg_ref[b] = bonus * g
        slabs.append((r, k, v, logw, a, bm))
    chunks = [tuple(x[n * c:(n + 1) * c] for x in slab) for slab in slabs for n in range(nc)]
    for i, (ar, lv, pm, kb_end, decay) in enumerate(_rwkv_chunk_prep(chunks, c)):
        b, n = divmod(i, nc)
        ar_ref[b, n] = ar
        lv_ref[b, n] = lv
        pm_ref[b, n] = pm
        kbe_ref[b, n] = kb_end
        dec_ref[b, n] = decay


def _rwkv_apply_kernel(ar_ref, lv_ref, pm_ref, kbe_ref, dec_ref, v_ref, g_ref, bg_ref, s0_ref, lng_ref, lnb_ref,
                       y_ref, s_ref, *, bsz, nc, c):
    @pl.when(pl.program_id(0) == 0)
    def _():
        s_ref[...] = s0_ref[...]

    states = [s_ref[b] for b in range(bsz)]
    for n in range(nc):
        rows = slice(n * c, (n + 1) * c)
        preps = [(ar_ref[b, n], lv_ref[b, n], pm_ref[b, n], kbe_ref[b, n], dec_ref[b, n]) for b in range(bsz)]
        ys, states = _rwkv_chunk_apply(states, preps, [v_ref[b, rows, :] for b in range(bsz)], c)
        for b in range(bsz):
            y_ref[b, rows, :] = _rwkv_post(ys[b], g_ref[b, rows, :], bg_ref[b, rows, :], lng_ref[...], lnb_ref[...])
    for b in range(bsz):
        s_ref[b] = states[b]


def _rwkv_long(zr, bsz, t, shift0, s0, rp, l, c, nc):
    rows = nc * c
    n_ch, n_steps = t // c, t // rows
    vec = lambda w: _layer_spec((1, w), l)
    z_specs = [pl.BlockSpec((rows, RW_IN), lambda j, b=b: (b * n_steps + j, 0)) for b in range(bsz)]
    tail_specs = [pl.BlockSpec((8, RW_IN), lambda j, b=b: (jnp.maximum((b * t + j * rows) // 8 - 1, 0), 0))
                  for b in range(bsz)]
    chunked = lambda h, w, dt: jax.ShapeDtypeStruct((bsz, n_ch, h, w), dt)
    chunk_out = lambda h, w: pl.BlockSpec((bsz, nc, h, w), lambda j: (0, j, 0, 0))
    tok_out = lambda: pl.BlockSpec((bsz, rows, BW), lambda j: (0, j, 0))
    ar, lv, pm, kbe, dec, v16, g, bg = pl.pallas_call(
        functools.partial(_rwkv_prep_kernel, bsz=bsz, nc=nc, c=c),
        out_shape=(chunked(2 * c, BW, BF16), chunked(2 * c, BW, F32), chunked(2 * c, RW_H * c, BF16),
                   chunked(2 * c, BW, BF16), chunked(1, BW, F32),
                   jax.ShapeDtypeStruct((bsz, t, BW), BF16), jax.ShapeDtypeStruct((bsz, t, BW), F32),
                   jax.ShapeDtypeStruct((bsz, t, BW), F32)),
        grid=(n_steps,),
        in_specs=z_specs + tail_specs + [_const_spec((bsz, 1, RW_IN))] + _rw_param_specs(l),
        out_specs=(chunk_out(2 * c, BW), chunk_out(2 * c, BW), chunk_out(2 * c, RW_H * c), chunk_out(2 * c, BW),
                   chunk_out(1, BW), tok_out(), tok_out(), tok_out()),
        scratch_shapes=[pltpu.VMEM((8 + rows, RW_IN), F32)],
        compiler_params=_params(("parallel",)),
        name="rwkv7_prep",
    )(*([zr] * (2 * bsz)), shift0, *[rp[n] for n in _RW_PARAMS])
    nca = _pick(n_ch, (2, 1))
    chunk_in = lambda h, w: pl.BlockSpec((bsz, nca, h, w), lambda j: (0, j, 0, 0))
    tok_in = lambda: pl.BlockSpec((bsz, nca * c, BW), lambda j: (0, j, 0))
    y, s_new = pl.pallas_call(
        functools.partial(_rwkv_apply_kernel, bsz=bsz, nc=nca, c=c),
        out_shape=(jax.ShapeDtypeStruct((bsz, t, BW), F32), jax.ShapeDtypeStruct((bsz, BW, BW), F32)),
        grid=(n_ch // nca,),
        in_specs=[chunk_in(2 * c, BW), chunk_in(2 * c, BW), chunk_in(2 * c, RW_H * c), chunk_in(2 * c, BW),
                  chunk_in(1, BW), tok_in(), tok_in(), tok_in(), _const_spec((bsz, BW, BW)), vec(BW), vec(BW)],
        out_specs=(tok_in(), _const_spec((bsz, BW, BW))),
        compiler_params=_params(("arbitrary",)),
        name="rwkv7_apply",
    )(ar, lv, pm, kbe, dec, v16, g, bg, s0, rp["ln_g"], rp["ln_b"])
    return y, s_new


def _conv_kernel(u_ref, buf_ref, w_ref, cb_ref, g_ref, b_ref, y_ref, nb_ref, xp_ref, *, tt):
    j = pl.program_id(1)
    halo = CONV_K - 1
    top = 32 - halo

    @pl.when(j == 0)
    def _():
        xp_ref[:, top:32, :] = buf_ref[...]

    xp_ref[:, 32:32 + tt, :] = u_ref[...].reshape(xp_ref.shape[0], tt, BW)
    acc = xp_ref[:, top:top + tt, :] * w_ref[0:1, :] + cb_ref[...]
    for k in range(1, CONV_K):
        acc = acc + xp_ref[:, top + k:top + k + tt, :] * w_ref[k:k + 1, :]
    yn = _layer_norm(acc, g_ref[...], b_ref[...])
    y_ref[...] = yn * _sigmoid(yn)
    new = xp_ref[:, top + tt:32 + tt, :]
    xp_ref[:, top:32, :] = new
    nb_ref[...] = new


def _conv(u, row0, bsz, t, buf, cp, l, bb, tt):
    assert bb == 1 or tt == t
    halo = CONV_K - 1
    nt = t // tt
    blk0 = row0 // (bb * tt)
    vec = lambda: _layer_spec((1, BW), l)
    return pl.pallas_call(
        functools.partial(_conv_kernel, tt=tt),
        out_shape=(jax.ShapeDtypeStruct((bsz, t, BW), F32), jax.ShapeDtypeStruct((bsz, halo, BW), F32)),
        grid=(bsz // bb, t // tt),
        in_specs=[pl.BlockSpec((bb * tt, BW), lambda i, j: (blk0 + i * nt + j, 0)),
                  pl.BlockSpec((bb, halo, BW), lambda i, j: (i, 0, 0)),
                  _layer_spec((CONV_K, BW), l), vec(), vec(), vec()],
        out_specs=(pl.BlockSpec((bb, tt, BW), lambda i, j: (i, j, 0)),
                   pl.BlockSpec((bb, halo, BW), lambda i, j: (i, 0, 0))),
        scratch_shapes=[pltpu.VMEM((bb, 32 + tt, BW), F32)],
        compiler_params=_params(("parallel", "arbitrary")),
        name="conv_module",
    )(u, buf, cp["w"], cp["b"], cp["ln_g"], cp["ln_b"])


def _gelu_tanh(x):
    return 0.5 * x * (1.0 + jnp.tanh(math.sqrt(2.0 / math.pi) * (x + 0.044715 * (x * x * x))))


def _s5_out(u, hre, him, cre_ref, cim_ref, d_ref, gw_ref, gb_ref):
    y = _dot(hre.astype(BF16), cre_ref[...]) - _dot(him.astype(BF16), cim_ref[...]) + d_ref[...] * u
    y = _gelu_tanh(y)
    return y * _sigmoid(_dot(y.astype(BF16), gw_ref[...]) + gb_ref[...])


def _s5_prompt_kernel(u_ref, bre_ref, bim_ref, pre_ref, pim_ref, cre_ref, cim_ref, d_ref, gw_ref, gb_ref,
                      y_ref, hre_ref, him_ref, xre_ref, xim_ref, *, tt):
    j = pl.program_id(1)

    @pl.when(j == 0)
    def _():
        hre_ref[...] = jnp.zeros_like(hre_ref)
        him_ref[...] = jnp.zeros_like(him_ref)

    u = u_ref[...]
    ub = u.astype(BF16)
    xre = _dot(ub, bre_ref[...])
    xim = _dot(ub, bim_ref[...])
    xre = xre.reshape(tt // 8, 8, S5_N)
    xim = xim.reshape(tt // 8, 8, S5_N)
    for s in range(3):
        d = 1 << s
        are = pre_ref[s]
        aim = pim_ref[s]
        sre = pltpu.roll(xre, d, 1)
        sim = pltpu.roll(xim, d, 1)
        xre, xim = xre + (are * sre - aim * sim), xim + (are * sim + aim * sre)
    xre_ref[...] = xre.reshape(tt, S5_N)
    xim_ref[...] = xim.reshape(tt, S5_N)
    t8re = pre_ref[3]
    t8im = pim_ref[3]

    def group(g, carry):
        cre, cim = carry
        rows = pl.ds(pl.multiple_of(g * 8, 8), 8)
        gre = xre_ref[rows, :] + (t8re * cre - t8im * cim)
        gim = xim_ref[rows, :] + (t8re * cim + t8im * cre)
        xre_ref[rows, :] = gre
        xim_ref[rows, :] = gim
        return (jnp.broadcast_to(gre[7:8, :], (8, S5_N)), jnp.broadcast_to(gim[7:8, :], (8, S5_N)))

    cre, cim = lax.fori_loop(0, tt // 8, group, (jnp.broadcast_to(hre_ref[...], (8, S5_N)),
                                                 jnp.broadcast_to(him_ref[...], (8, S5_N))))
    hre_ref[...] = cre[0:1, :]
    him_ref[...] = cim[0:1, :]
    y_ref[...] = _s5_out(u, xre_ref[...], xim_ref[...], cre_ref, cim_ref, d_ref, gw_ref, gb_ref)


def _s5_prompt(u, bsz, t, sp, l, tt):
    nt = t // tt
    return pl.pallas_call(
        functools.partial(_s5_prompt_kernel, tt=tt),
        out_shape=(jax.ShapeDtypeStruct((bsz, t, BW), F32),
                   jax.ShapeDtypeStruct((bsz, 1, S5_N), F32), jax.ShapeDtypeStruct((bsz, 1, S5_N), F32)),
        grid=(bsz, nt),
        in_specs=[pl.BlockSpec((tt, BW), lambda i, j: (i * nt + j, 0)),
                  _layer_spec((BW, S5_N), l), _layer_spec((BW, S5_N), l),
                  _layer_spec((4, 8, S5_N), l), _layer_spec((4, 8, S5_N), l),
                  _layer_spec((S5_N, BW), l), _layer_spec((S5_N, BW), l),
                  _layer_spec((1, BW), l), _layer_spec((BW, BW), l), _layer_spec((1, BW), l)],
        out_specs=(pl.BlockSpec((None, tt, BW), lambda i, j: (i, j, 0)),
                   pl.BlockSpec((None, 1, S5_N), lambda i, j: (i, 0, 0)),
                   pl.BlockSpec((None, 1, S5_N), lambda i, j: (i, 0, 0))),
        scratch_shapes=[pltpu.VMEM((tt, S5_N), F32), pltpu.VMEM((tt, S5_N), F32)],
        compiler_params=_params(("parallel", "arbitrary")),
        name="s5_prompt",
    )(u, sp["bre"], sp["bim"], sp["pre"], sp["pim"], sp["cre"], sp["cim"], sp["d"], sp["glu_w"], sp["glu_b"])


def _s5_sample_kernel(u_ref, h0re_ref, h0im_ref, bre_ref, bim_ref, pre_ref, pim_ref, cre_ref, cim_ref,
                      d_ref, gw_ref, gb_ref, y_ref, hre_ref, him_ref, *, ts):
    lre = pre_ref[3][0:1, :]
    lim = pim_ref[3][0:1, :]
    hre = h0re_ref[...]
    him = h0im_ref[...]
    for t in range(ts):
        u = u_ref[t]
        ub = u.astype(BF16)
        hre, him = (lre * hre - lim * him + _dot(ub, bre_ref[...]),
                    lre * him + lim * hre + _dot(ub, bim_ref[...]))
        y_ref[t] = _s5_out(u, hre, him, cre_ref, cim_ref, d_ref, gw_ref, gb_ref)
    hre_ref[...] = hre
    him_ref[...] = him


def _s5_sample(u_tm, h0re, h0im, sp, l):
    ts, bsz, _ = u_tm.shape
    return pl.pallas_call(
        functools.partial(_s5_sample_kernel, ts=ts),
        out_shape=(jax.ShapeDtypeStruct((ts, bsz, BW), F32),
                   jax.ShapeDtypeStruct((bsz, S5_N), F32), jax.ShapeDtypeStruct((bsz, S5_N), F32)),
        grid=(1,),
        in_specs=[_const_spec((ts, bsz, BW)), _const_spec((bsz, S5_N)), _const_spec((bsz, S5_N)),
                  _layer_spec((BW, S5_N), l), _layer_spec((BW, S5_N), l),
                  _layer_spec((4, 8, S5_N), l), _layer_spec((4, 8, S5_N), l),
                  _layer_spec((S5_N, BW), l), _layer_spec((S5_N, BW), l),
                  _layer_spec((1, BW), l), _layer_spec((BW, BW), l), _layer_spec((1, BW), l)],
        out_specs=(_const_spec((ts, bsz, BW)), _const_spec((bsz, S5_N)), _const_spec((bsz, S5_N))),
        compiler_params=_params(("arbitrary",)),
        name="s5_sample",
    )(u_tm, h0re, h0im, sp["bre"], sp["bim"], sp["pre"], sp["pim"], sp["cre"], sp["cim"], sp["d"],
      sp["glu_w"], sp["glu_b"])


def _merge_kernel(h_ref, *refs, p_tiles):
    yp_refs, ys_refs = refs[0:N_BRANCH], refs[N_BRANCH:2 * N_BRANCH]
    wg_ref, bg_ref, wbr_ref, wo_ref, g_ref, b_ref, o_ref = refs[2 * N_BRANCH:]
    is_prompt = pl.program_id(0) < p_tiles
    h = h_ref[...]
    hb = h.astype(BF16)
    m = None
    for n in range(N_BRANCH):
        y = jnp.where(is_prompt, yp_refs[n][...], ys_refs[n][...])
        gate = _sigmoid(_dot(hb, wg_ref[:, n * D_MODEL:(n + 1) * D_MODEL]) + bg_ref[n:n + 1, :])
        term = gate * _dot(y.astype(BF16), wbr_ref[n])
        m = term if m is None else m + term
    out = _dot(m.astype(BF16), wo_ref[...])
    o_ref[...] = _layer_norm(ALPHA * h + out, g_ref[...], b_ref[...])


def _merge(h, ys_p, ys_s, mw, l):
    n = h.shape[0]
    n_p, n_s = ys_p[0].shape[0], ys_s[0].shape[0]
    tm = _pick(math.gcd(n_p, n_s), (512, 256, 128, 64, 32, 16, 8))
    p_tiles, s_tiles = n_p // tm, n_s // tm
    row = lambda w: pl.BlockSpec((tm, w), lambda i: (i, 0))
    p_row = pl.BlockSpec((tm, BW), lambda i: (jnp.minimum(i, p_tiles - 1), 0))
    s_row = pl.BlockSpec((tm, BW), lambda i: (jnp.clip(i - p_tiles, 0, s_tiles - 1), 0))
    return pl.pallas_call(
        functools.partial(_merge_kernel, p_tiles=p_tiles),
        out_shape=jax.ShapeDtypeStruct((n, D_MODEL), F32),
        grid=(n // tm,),
        in_specs=[row(D_MODEL)] + [p_row] * N_BRANCH + [s_row] * N_BRANCH + [
                  _layer_spec((D_MODEL, N_BRANCH * D_MODEL), l), _layer_spec((N_BRANCH, D_MODEL), l),
                  _layer_spec((N_BRANCH, BW, D_MODEL), l), _layer_spec((D_MODEL, D_MODEL), l),
                  _layer_spec((1, D_MODEL), l), _layer_spec((1, D_MODEL), l)],
        out_specs=row(D_MODEL),
        compiler_params=_params(("parallel",)),
        name="branch_merge",
    )(h, *ys_p, *ys_s, mw["wg"], mw["bg"], mw["wbr"], mw["wo"], mw["g"], mw["b"])


def _pad_rows(w, lo, total):
    return jnp.pad(w, ((0, 0), (lo, total - lo - w.shape[1]), (0, 0)))


def _s5_params(a_re, a_im, log_dt, b_re, b_im, c_re, c_im):
    dep = a_re.shape[0]
    dt = jnp.exp(log_dt)[..., None]
    mag = jnp.exp(a_re * dt)
    lam_re, lam_im = mag * jnp.cos(a_im * dt), mag * jnp.sin(a_im * dt)
    den = a_re * a_re + a_im * a_im
    co_re = ((lam_re - 1.0) * a_re + lam_im * a_im) / den
    co_im = (lam_im * a_re - (lam_re - 1.0) * a_im) / den
    bb_re = co_re[..., None] * b_re - co_im[..., None] * b_im
    bb_im = co_re[..., None] * b_im + co_im[..., None] * b_re
    eye = jnp.eye(S5_GROUPS, dtype=F32)
    def bmat(x):
        return jnp.einsum("lgpc,gk->lgckp", x, eye).reshape(dep, BW, S5_N)
    def cmat(x):
        return jnp.einsum("lgcp,gk->lgpkc", x, eye).reshape(dep, S5_N, BW)
    lr, li = lam_re.reshape(dep, 1, S5_N), lam_im.reshape(dep, 1, S5_N)
    row = jnp.arange(8)[None, :, None]
    pre, pim = [], []
    xr, xi = lr, li
    for s in range(3):
        pre.append(jnp.where(row >= (1 << s), xr, 0.0))
        pim.append(jnp.where(row >= (1 << s), xi, 0.0))
        xr, xi = xr * xr - xi * xi, 2.0 * xr * xi
    seq_r, seq_i = [lr], [li]
    for _ in range(7):
        xr, xi = seq_r[-1], seq_i[-1]
        seq_r.append(xr * lr - xi * li)
        seq_i.append(xr * li + xi * lr)
    pre.append(jnp.concatenate(seq_r, axis=1))
    pim.append(jnp.concatenate(seq_i, axis=1))
    return dict(bre=bmat(bb_re).astype(BF16), bim=bmat(bb_im).astype(BF16),
                pre=jnp.stack(pre, axis=1), pim=jnp.stack(pim, axis=1),
                cre=cmat(c_re).astype(BF16), cim=cmat(c_im).astype(BF16))


def _rope_tables(pos):
    inv = 1.0 / (ROPE_BASE ** (jnp.arange(0, MLA_ROPE, 2, dtype=F32) / MLA_ROPE))
    ang = pos.astype(F32)[:, None] * inv[None, :]
    cos, sin = jnp.cos(ang), jnp.sin(ang)
    return jnp.concatenate([cos, cos], -1), jnp.concatenate([-sin, sin], -1)


def kernel(x_prompt, x_sample, cache_ckv, cache_kpe, state_rwkv, state_rwkv_shift, state_conv, state_s5_re, state_s5_im, page_table, ffn1_w_in, ffn1_w_down, ln1_g, ln1_b, w_in, b_gate, mla_q_norm, mla_w_uq, mla_kv_norm, mla_w_uk, mla_w_uv, rwkv_mu, rwkv_w0, rwkv_w2, rwkv_a0, rwkv_a2, rwkv_g2, rwkv_k_k, rwkv_k_a, rwkv_r_k, rwkv_ln_g, rwkv_ln_b, conv_w, conv_b, conv_ln_g, conv_ln_b, s5_a_re, s5_a_im, s5_log_dt, s5_b_re, s5_b_im, s5_c_re, s5_c_im, s5_d, s5_glu_w, s5_glu_b, w_branch, w_out, ln2_g, ln2_b, ffn2_w_in, ffn2_w_down, ln3_g, ln3_b):
    bp, tp, _ = x_prompt.shape
    bs, ts, _ = x_sample.shape
    depth = w_in.shape[0]
    n_p, n_s = bp * tp, bs * ts
    past = page_table.shape[1] * PAGE
    halo = CONV_K - 1
    vec = lambda w: w[:, None, :]

    o_kv = MLA_Q
    o_kp = o_kv + MLA_KV
    o_r = o_kp + MLA_ROPE
    o_c = o_r + RW_IN
    o_s = o_c + 2 * BW
    o_g = o_s + BW
    half = MLA_ROPE // 2
    wkp = w_in[:, :, o_kp:o_r]
    uq = mla_w_uq.reshape(depth, MLA_Q, MLA_H, MLA_NOPE + MLA_ROPE)
    uq_pe = uq[..., MLA_NOPE:]
    swap = lambda x: jnp.concatenate([x[..., half:], x[..., :half]], -1)
    to_heads = lambda x: jnp.moveaxis(x, 2, 1).astype(BF16)
    pw = dict(wq=w_in[:, :, :o_kv].astype(BF16), wkv=w_in[:, :, o_kv:o_kp].astype(BF16),
              wkp=wkp.astype(BF16), wkps=swap(wkp).astype(BF16),
              wr=w_in[:, :, o_r:o_c].astype(BF16), wc=w_in[:, :, o_c:o_s].astype(BF16),
              ws=w_in[:, :, o_s:o_g].astype(BF16),
              gq=vec(mla_q_norm), gkv=vec(mla_kv_norm),
              wn=to_heads(uq[..., :MLA_NOPE]), wp=to_heads(uq_pe), wps=to_heads(swap(uq_pe)),
              wuk=jnp.transpose(mla_w_uk, (0, 2, 3, 1)).astype(BF16))
    eye_h = jnp.eye(MLA_H, dtype=F32)
    wuv = jnp.einsum("lrhv,hk->lhrkv", mla_w_uv, eye_h).reshape(depth, MLA_H, MLA_KV, BW).astype(BF16)
    rp = dict(mu=vec(rwkv_mu), w0=vec(rwkv_w0), a0=vec(rwkv_a0),
              w2=_pad_rows(rwkv_w2, 0, 128).astype(BF16), a2=_pad_rows(rwkv_a2, 32, 128).astype(BF16),
              g2=_pad_rows(rwkv_g2, 64, 128).astype(BF16),
              k_k=vec(rwkv_k_k), k_a=vec(rwkv_k_a), r_k=vec(rwkv_r_k.reshape(depth, BW)),
              ln_g=vec(rwkv_ln_g), ln_b=vec(rwkv_ln_b))
    cp = dict(w=conv_w, b=vec(conv_b), ln_g=vec(conv_ln_g), ln_b=vec(conv_ln_b))
    tt_s5 = _pick(tp, (512, 256, 128))
    sp = _s5_params(s5_a_re, s5_a_im, s5_log_dt, s5_b_re, s5_b_im, s5_c_re, s5_c_im)
    sp.update(d=vec(s5_d), glu_w=s5_glu_w.astype(BF16), glu_b=vec(s5_glu_b))
    mw = dict(wg=w_in[:, :, o_g:].astype(BF16), bg=b_gate, wbr=w_branch.astype(BF16), wo=w_out.astype(BF16),
              g=vec(ln2_g), b=vec(ln2_b))
    f1 = (ffn1_w_in.astype(BF16), ffn1_w_down.astype(BF16), vec(ln1_g), vec(ln1_b))
    f2 = (ffn2_w_in.astype(BF16), ffn2_w_down.astype(BF16), vec(ln3_g), vec(ln3_b))

    cos_p, sin_p = _rope_tables(jnp.arange(tp))
    cos_s, sin_s = _rope_tables(past + jnp.arange(ts))
    cos2 = jnp.concatenate([jnp.tile(cos_p, (bp, 1)), jnp.tile(cos_s, (bs, 1))], 0)
    sin2 = jnp.concatenate([jnp.tile(sin_p, (bp, 1)), jnp.tile(sin_s, (bs, 1))], 0)

    c_rw = _pick(tp, (64, 32, 16, 8))
    nc_rw = _pick(tp // c_rw, (4, 2, 1))
    bb_rw = _pick(bs, (8, 4, 2, 1))
    tt_cv = _pick(tp, (512, 256, 128, 64, 32))
    bb_cv = _pick(bs, (32, 16, 8, 4, 2, 1))

    cache_kpe_t = jnp.swapaxes(cache_kpe, 2, 3)
    x = jnp.concatenate([x_prompt.reshape(n_p, D_MODEL), x_sample.reshape(n_s, D_MODEL)], 0)
    outs = [[] for _ in range(14)]
    for l in range(depth):
        h = _ffn_ln(x, *f1, l)
        qcat, q16, ckv, kpe, kcat, zr, u, zs = _prep(h, cos2, sin2, pw, l)
        ya_p = _attn_prompt(q16, kcat, wuv, l, bp, tp)
        ya_s = _attn_sample(page_table, qcat, ckv, kpe, cache_ckv, cache_kpe_t, wuv, l, n_p, bs, ts)
        yb_p, srw_p = _rwkv_long(zr, bp, tp, jnp.zeros((bp, 1, RW_IN), F32), jnp.zeros((bp, BW, BW), F32),
                                 rp, l, c_rw, nc_rw)
        yb_s, srw_s = _rwkv_fused(zr, n_p, bs, ts, state_rwkv_shift[l][:, None, :],
                                  state_rwkv[l].reshape(bs, BW, RW_HD), rp, l, bb_rw)
        yc_p, cv_p = _conv(u, 0, bp, tp, jnp.zeros((bp, halo, BW), F32), cp, l, 1, tt_cv)
        yc_s, cv_s = _conv(u, n_p, bs, ts, state_conv[l], cp, l, bb_cv, ts)
        yd_p, s5re_p, s5im_p = _s5_prompt(zs, bp, tp, sp, l, tt_s5)
        yd_s, s5re_s, s5im_s = _s5_sample(jnp.swapaxes(zs[n_p:].reshape(bs, ts, BW), 0, 1),
                                          state_s5_re[l].reshape(bs, S5_N), state_s5_im[l].reshape(bs, S5_N), sp, l)
        yd_s = jnp.swapaxes(yd_s, 0, 1)
        flat = lambda a: a.reshape(-1, BW)
        x = _merge(h, (ya_p, flat(yb_p), flat(yc_p), flat(yd_p)), (ya_s, flat(yb_s), flat(yc_s), flat(yd_s)),
                   mw, l)
        x = _ffn_ln(x, *f2, l)
        shift_p = zr[tp - 1:n_p:tp]
        shift_s = zr[n_p + ts - 1::ts]

        diag = lambda s: jnp.stack([s[:, i * RW_HD:(i + 1) * RW_HD, i * RW_HD:(i + 1) * RW_HD]
                                    for i in range(RW_H)], 1)
        new_p = (ckv[:n_p].reshape(bp, tp, MLA_KV), kpe[:n_p].reshape(bp, tp, MLA_ROPE), diag(srw_p),
                 shift_p, cv_p,s5re_p.reshape(bp, S5_GROUPS, S5_P), s5im_p.reshape(bp, S5_GROUPS, S5_P))
        new_s = (ckv[n_p:].reshape(bs, ts, MLA_KV), kpe[n_p:].reshape(bs, ts, MLA_ROPE),
                 srw_s.reshape(bs, RW_H, RW_HD, RW_HD),
                 shift_s, cv_s,s5re_s.reshape(bs, S5_GROUPS, S5_P), s5im_s.reshape(bs, S5_GROUPS, S5_P))
        for i, a in enumerate(new_p + new_s):
            outs[i].append(a)
    return (x[:n_p].reshape(bp, tp, D_MODEL), x[n_p:].reshape(bs, ts, D_MODEL)) + tuple(jnp.stack(o) for o in outs)
```
